```python
import math
import jax, jax.numpy as jnp
from jax import lax
import numpy as np

D_MODEL = 2048
BATCH = 4
SEQ = 2048
DEPTH = 1

SGU_GROUP_DIM = 128
SGU_WIDTH = D_MODEL // 2
SGU_GROUPS = SGU_WIDTH // SGU_GROUP_DIM
CHUNK = 128
HEAD_DIM = 128
N_HEADS = (D_MODEL // 2) // HEAD_DIM
N_KV_HEADS = 2
GQA_GROUP = N_HEADS // N_KV_HEADS
ATT_WIDTH = N_HEADS * HEAD_DIM
KV_WIDTH = N_KV_HEADS * HEAD_DIM
WINDOW = 128
BLOCK = 128
REL_BUCKETS = 32
REL_MAX_DIST = 128
D_FF = ((8 * D_MODEL // 3 + 255) // 256) * 256
EPS = 1e-6
NEG = -1e30

IN_SPLITS = [SGU_WIDTH, SGU_WIDTH, ATT_WIDTH, KV_WIDTH, KV_WIDTH, D_MODEL, D_MODEL]
IN_COLS = int(sum(IN_SPLITS))
IN_OFFSETS = [int(o) for o in np.cumsum(IN_SPLITS)[:-1]]

kernel_name = "hybrid_sgu_swa_gated_encoder"


def rms_norm(x, g):
    xf = x.astype(jnp.float32)
    y = xf * lax.rsqrt(jnp.mean(xf * xf, axis=-1, keepdims=True) + EPS)
    return (y * g.astype(jnp.float32)).astype(x.dtype)


def t5_bucket(rel):
    nb = REL_BUCKETS // 2
    ret = jnp.where(rel > 0, nb, 0)
    n = jnp.abs(rel)
    max_exact = nb // 2
    nf = jnp.maximum(n, 1).astype(jnp.float32)
    large = max_exact + (jnp.log(nf / max_exact) / math.log(REL_MAX_DIST / max_exact)
                         * (nb - max_exact)).astype(jnp.int32)
    large = jnp.minimum(large, nb - 1)
    return ret + jnp.where(n < max_exact, n, large)


def band_structure(seq):
    nblk = seq // BLOCK
    qi = jnp.arange(BLOCK)[:, None]
    kj = jnp.arange(3 * BLOCK)[None, :]
    rel = kj - BLOCK - qi
    key_pos = jnp.arange(nblk)[:, None, None] * BLOCK + kj[None] - BLOCK
    valid = (jnp.abs(rel)[None] <= WINDOW) & (key_pos >= 0) & (key_pos < seq)
    return rel, valid


def band(t, nblk):
    tp = jnp.pad(t, ((0, 0), (BLOCK, BLOCK), (0, 0), (0, 0)))
    tp = tp.reshape(t.shape[0], nblk + 2, BLOCK, t.shape[2], t.shape[3])
    return jnp.concatenate([tp[:, :-2], tp[:, 1:-1], tp[:, 2:]], axis=2)


def windowed_gqa(q, k, v, rel_bias, sink):
    B, S = q.shape[0], q.shape[1]
    nb = S // BLOCK
    q = q.reshape(B, nb, BLOCK, N_KV_HEADS, GQA_GROUP, HEAD_DIM)
    kb = band(k.reshape(B, S, N_KV_HEADS, HEAD_DIM), nb)
    vb = band(v.reshape(B, S, N_KV_HEADS, HEAD_DIM), nb)
    s = jnp.einsum('bnqkgd,bnjkd->bnkgqj', q, kb).astype(jnp.float32) * (HEAD_DIM ** -0.5)
    rel, valid = band_structure(S)
    bias = rel_bias.astype(jnp.float32)[t5_bucket(rel)]
    bias = bias.transpose(2, 0, 1).reshape(N_KV_HEADS, GQA_GROUP, BLOCK, 3 * BLOCK)
    s = jnp.where(valid[None, :, None, None], s + bias, NEG)
    sink_logit = jnp.broadcast_to(
        sink.astype(jnp.float32).reshape(N_KV_HEADS, GQA_GROUP)[None, None, :, :, None, None],
        s.shape[:-1] + (1,))
    p = jax.nn.softmax(jnp.concatenate([s, sink_logit], axis=-1), axis=-1)[..., :-1]
    o = jnp.einsum('bnkgqj,bnjkd->bnqkgd', p.astype(vb.dtype), vb)
    return o.reshape(B, S, ATT_WIDTH)


def chunked_sgu(u, v, v_gain, w_s, b_s):
    B, S = u.shape[0], u.shape[1]
    nc = S // CHUNK
    v = rms_norm(v, v_gain).reshape(B, nc, CHUNK, SGU_GROUPS, SGU_GROUP_DIM)
    mixed = jnp.einsum('gpq,bcqge->bcpge', w_s, v) + b_s.T[:, :, None]
    return u * mixed.reshape(B, S, SGU_WIDTH)


def setup_inputs(seed: int = 0) -> dict:
    key = jax.random.key(seed)
    ks = jax.random.split(key, 20)
    f32 = jnp.float32

    def nrm(k, shape, scale):
        return jax.random.normal(k, shape, f32) * scale

    return {
        "x": nrm(ks[0], (BATCH, SEQ, D_MODEL), 1.0),
        "w_in": nrm(ks[1], (DEPTH, D_MODEL, IN_COLS), D_MODEL ** -0.5),
        "norm_mix": 1.0 + nrm(ks[2], (DEPTH, D_MODEL), 0.05),
        "sgu_v_gain": 1.0 + nrm(ks[3], (DEPTH, SGU_WIDTH), 0.05),
        "sgu_w_s": nrm(ks[4], (DEPTH, SGU_GROUPS, CHUNK, CHUNK), 0.5 * CHUNK ** -0.5),
        "sgu_b_s": 1.0 + nrm(ks[5], (DEPTH, SGU_GROUPS, CHUNK), 0.1),
        "w_a_out": nrm(ks[6], (DEPTH, SGU_WIDTH, D_MODEL), SGU_WIDTH ** -0.5),
        "attn_sink": nrm(ks[7], (DEPTH, N_HEADS), 0.5),
        "rel_bias": nrm(ks[8], (REL_BUCKETS, N_HEADS), 0.5),
        "w_b_out": nrm(ks[9], (DEPTH, ATT_WIDTH, D_MODEL), ATT_WIDTH ** -0.5),
        "w_o": nrm(ks[10], (DEPTH, D_MODEL, D_MODEL), D_MODEL ** -0.5),
        "norm_ffn": 1.0 + nrm(ks[11], (DEPTH, D_MODEL), 0.05),
        "w_gate": nrm(ks[12], (DEPTH, D_MODEL, D_FF), D_MODEL ** -0.5),
        "w_up": nrm(ks[13], (DEPTH, D_MODEL, D_FF), D_MODEL ** -0.5),
        "w_down": nrm(ks[14], (DEPTH, D_FF, D_MODEL), D_FF ** -0.5),
        "norm_final": 1.0 + nrm(ks[15], (D_MODEL,), 0.05),
    }


def reference(x, w_in, norm_mix, sgu_v_gain, sgu_w_s, sgu_b_s, w_a_out, attn_sink, rel_bias,
              w_b_out, w_o, norm_ffn, w_gate, w_up, w_down, norm_final):
    for l in range(DEPTH):
        h = rms_norm(x, norm_mix[l])
        z = h @ w_in[l]
        zu, zv, q, k, v, ga, gb = jnp.split(z, IN_OFFSETS, axis=-1)
        y_a = chunked_sgu(jax.nn.gelu(zu), jax.nn.gelu(zv), sgu_v_gain[l],
                          sgu_w_s[l], sgu_b_s[l]) @ w_a_out[l]
        y_b = windowed_gqa(q, k, v, rel_bias, attn_sink[l]) @ w_b_out[l]
        m = jax.nn.sigmoid(ga) * y_a + jax.nn.sigmoid(gb) * y_b
        x = x + m @ w_o[l]
        h = rms_norm(x, norm_ffn[l])
        x = x + (jax.nn.silu(h @ w_gate[l]) * (h @ w_up[l])) @ w_down[l]
    return rms_norm(x, norm_final)
```

```python
import functools
import math

import jax
import jax.numpy as jnp
import numpy as np
from jax import lax
from jax.experimental import pallas as pl
from jax.experimental.pallas import tpu as pltpu

LANES_V7X = 128
VMEM_BYTES_V7X = 64 * 1024 * 1024

D_MODEL = 2048
SGU_WIDTH = D_MODEL // 2
SGU_GROUP_DIM = 128
SGU_GROUPS = SGU_WIDTH // SGU_GROUP_DIM
CHUNK = 128
HEAD_DIM = 128
N_HEADS = (D_MODEL // 2) // HEAD_DIM
N_KV_HEADS = 2
GQA_GROUP = N_HEADS // N_KV_HEADS
ATT_WIDTH = N_HEADS * HEAD_DIM
KV_WIDTH = N_KV_HEADS * HEAD_DIM
WINDOW = 128
BLOCK = 128
BAND = 3 * BLOCK
REL_BUCKETS = 32
REL_MAX_DIST = 128
D_FF = ((8 * D_MODEL // 3 + 255) // 256) * 256
EPS = 1e-6
NEG = -1e30

GATE_COLS = 2 * D_MODEL
MAIN_COLS = 2 * SGU_WIDTH + ATT_WIDTH + 2 * KV_WIDTH
IN_COLS = GATE_COLS + MAIN_COLS
OFF_U = GATE_COLS
OFF_V = OFF_U + SGU_WIDTH
OFF_Q = OFF_V + SGU_WIDTH
OFF_K = OFF_Q + ATT_WIDTH
OFF_VA = OFF_K + KV_WIDTH

BF16 = jnp.bfloat16
F32 = jnp.float32


def _vmem_limit(nbytes):
    return int(min(nbytes, VMEM_BYTES_V7X - 4 * 1024 * 1024))


def _rms_scale(x):
    return lax.rsqrt(jnp.mean(x * x, axis=-1, keepdims=True) + EPS)


IN_TM = 1024
IN_TN = 1536
IN_ROWS = 256


def _in_proj_kernel(x_ref, g_ref, w_ref, z_ref, h_sc):
    @pl.when(pl.program_id(1) == 0)
    def _():
        def body(r, carry):
            rows = pl.ds(pl.multiple_of(r * IN_ROWS, IN_ROWS), IN_ROWS)
            x = x_ref[rows, :]
            h_sc[rows, :] = (x * _rms_scale(x) * g_ref[...]).astype(BF16)
            return carry
        lax.fori_loop(0, IN_TM // IN_ROWS, body, 0)

    z_ref[...] = jnp.dot(h_sc[...], w_ref[...], preferred_element_type=F32).astype(BF16)


def _in_proj(x2, g, w):
    m = x2.shape[0]
    vmem = 2 * IN_TM * D_MODEL * 4 + 2 * D_MODEL * IN_TN * 2 + 2 * IN_TM * IN_TN * 2 \
        + IN_TM * D_MODEL * 2 + IN_TM * IN_TN * 4 + (4 << 20)
    return pl.pallas_call(
        _in_proj_kernel,
        out_shape=jax.ShapeDtypeStruct((m, IN_COLS), BF16),
        grid=(m // IN_TM, IN_COLS // IN_TN),
        in_specs=[
            pl.BlockSpec((IN_TM, D_MODEL), lambda i, j: (i, 0)),
            pl.BlockSpec((1, D_MODEL), lambda i, j: (0, 0)),
            pl.BlockSpec((D_MODEL, IN_TN), lambda i, j: (0, j)),
        ],
        out_specs=pl.BlockSpec((IN_TM, IN_TN), lambda i, j: (i, j)),
        scratch_shapes=[pltpu.VMEM((IN_TM, D_MODEL), BF16)],
        compiler_params=pltpu.CompilerParams(
            dimension_semantics=("arbitrary", "arbitrary"),
            vmem_limit_bytes=_vmem_limit(vmem)),
        name="in_proj",
    )(x2, g, w)


def _band_bucket_table():
    qi = jnp.arange(BLOCK)[:, None]
    kj = jnp.arange(BAND)[None, :]
    rel = kj - BLOCK - qi
    nb = REL_BUCKETS // 2
    ret = jnp.where(rel > 0, nb, 0)
    n = jnp.abs(rel)
    max_exact = nb // 2
    nf = jnp.maximum(n, 1).astype(jnp.float32)
    large = max_exact + (jnp.log(nf / max_exact) / math.log(REL_MAX_DIST / max_exact)
                         * (nb - max_exact)).astype(jnp.int32)
    large = jnp.minimum(large, nb - 1)
    bucket = ret + jnp.where(n < max_exact, n, large)
    return jnp.where(n <= WINDOW, bucket, -1).astype(jnp.int32)


def _mixers_kernel(bucket_ref, relb_ref, sink_ref, gain_ref, ws_ref, bs_ref,
                   u_ref, v_ref, q_ref, kp_ref, kc_ref, kn_ref, vp_ref, vc_ref, vn_ref,
                   a_ref, o_ref, bias_sc, bsb_sc, kband_sc, vband_sc):
    b = pl.program_id(0)
    blk = pl.program_id(1)
    nblk = pl.num_programs(1)

    @pl.when((b == 0) & (blk == 0))
    def _():
        bucket = bucket_ref[...]
        for h in range(N_HEADS):
            acc = jnp.full((BLOCK, BAND), NEG, F32)
            for bk in range(REL_BUCKETS):
                acc = jnp.where(bucket == bk, relb_ref[bk, h], acc)
            bias_sc[h] = acc
        for g in range(SGU_GROUPS):
            bsb_sc[g] = jnp.broadcast_to(bs_ref[:, g:g + 1], (CHUNK, SGU_GROUP_DIM))

    u = jax.nn.gelu(u_ref[...].astype(F32))
    v = jax.nn.gelu(v_ref[...].astype(F32))
    vn = (v * _rms_scale(v) * gain_ref[...]).astype(BF16)
    for g in range(SGU_GROUPS):
        cols = slice(g * SGU_GROUP_DIM, (g + 1) * SGU_GROUP_DIM)
        mixed = jnp.dot(ws_ref[g], vn[:, cols], preferred_element_type=F32) + bsb_sc[g]
        a_ref[:, cols] = (u[:, cols] * mixed).astype(BF16)

    kband_sc[0:BLOCK, :] = kp_ref[...]
    kband_sc[BLOCK:2 * BLOCK, :] = kc_ref[...]
    kband_sc[2 * BLOCK:, :] = kn_ref[...]
    vband_sc[0:BLOCK, :] = vp_ref[...]
    vband_sc[BLOCK:2 * BLOCK, :] = vc_ref[...]
    vband_sc[2 * BLOCK:, :] = vn_ref[...]

    col = lax.broadcasted_iota(jnp.int32, (1, BAND), 1)
    outside = ((col < BLOCK) & (blk == 0)) | ((col >= 2 * BLOCK) & (blk == nblk - 1))
    scale = HEAD_DIM ** -0.5
    for kv in range(N_KV_HEADS):
        kcols = slice(kv * HEAD_DIM, (kv + 1) * HEAD_DIM)
        heads = range(kv * GQA_GROUP, (kv + 1) * GQA_GROUP)
        q4 = jnp.concatenate([q_ref[:, h * HEAD_DIM:(h + 1) * HEAD_DIM] for h in heads], axis=0)
        s = lax.dot_general(q4, kband_sc[:, kcols], (((1,), (1,)), ((), ())),
                            preferred_element_type=F32)
        bias4 = jnp.concatenate([bias_sc[h] for h in heads], axis=0)
        s = jnp.where(outside, NEG, s * scale + bias4)
        sink4 = jnp.concatenate([jnp.full((BLOCK, 1), sink_ref[h], F32) for h in heads], axis=0)
        mx = jnp.maximum(jnp.max(s, axis=-1, keepdims=True), sink4)
        p = jnp.exp(s - mx)
        denom = jnp.sum(p, axis=-1, keepdims=True) + jnp.exp(sink4 - mx)
        pv = jnp.dot(p.astype(BF16), vband_sc[:, kcols], preferred_element_type=F32)
        o4 = pv * (1.0 / denom)
        for n, h in enumerate(heads):
            o_ref[:, h * HEAD_DIM:(h + 1) * HEAD_DIM] = o4[n * BLOCK:(n + 1) * BLOCK].astype(BF16)


def _mixers(z, bucket, rel_bias, sink, gain, w_s, b_s_t, batch, seq):
    m = z.shape[0]
    nblk = seq // BLOCK

    def row(b, i):
        return b * nblk + i

    def zspec(width, off):
        return pl.BlockSpec((BLOCK, width), lambda b, i: (row(b, i), off // width))

    def halo(width, off, d):
        return pl.BlockSpec(
            (BLOCK, width),
            lambda b, i: (row(b, jnp.clip(i + d, 0, nblk - 1)), off // width))

    smem = pl.BlockSpec(memory_space=pltpu.SMEM)
    full = lambda shape: pl.BlockSpec(shape, lambda b, i: (0,) * len(shape))
    return pl.pallas_call(
        _mixers_kernel,
        out_shape=(jax.ShapeDtypeStruct((m, SGU_WIDTH), BF16),
                   jax.ShapeDtypeStruct((m, ATT_WIDTH), BF16)),
        grid=(batch, nblk),
        in_specs=[
            full((BLOCK, BAND)), smem, smem, full((1, SGU_WIDTH)),
            full((SGU_GROUPS, CHUNK, CHUNK)), full((CHUNK, SGU_GROUPS)),
            zspec(SGU_WIDTH, OFF_U), zspec(SGU_WIDTH, OFF_V), zspec(ATT_WIDTH, OFF_Q),
            halo(KV_WIDTH, OFF_K, -1), zspec(KV_WIDTH, OFF_K), halo(KV_WIDTH, OFF_K, 1),
            halo(KV_WIDTH, OFF_VA, -1), zspec(KV_WIDTH, OFF_VA), halo(KV_WIDTH, OFF_VA, 1),
        ],
        out_specs=(pl.BlockSpec((BLOCK, SGU_WIDTH), lambda b, i: (row(b, i), 0)),
                   pl.BlockSpec((BLOCK, ATT_WIDTH), lambda b, i: (row(b, i), 0))),
        scratch_shapes=[
            pltpu.VMEM((N_HEADS, BLOCK, BAND), F32),
            pltpu.VMEM((SGU_GROUPS, CHUNK, SGU_GROUP_DIM), F32),
            pltpu.VMEM((BAND, KV_WIDTH), BF16),
            pltpu.VMEM((BAND, KV_WIDTH), BF16),
        ],
        compiler_params=pltpu.CompilerParams(
            dimension_semantics=("arbitrary", "arbitrary"),
            vmem_limit_bytes=_vmem_limit(32 << 20)),
        name="mixers",
    )(bucket, rel_bias, sink, gain, w_s, b_s_t, z, z, z, z, z, z, z, z, z)


MG_TM = 256
MG_TN = 512


def _merge_kernel(a_ref, o_ref, ga_ref, gb_ref, x_ref, wa_ref, wb_ref, wo_ref, x1_ref, m_sc):
    a = a_ref[...]
    o = o_ref[...]
    for n in range(D_MODEL // MG_TN):
        cols = slice(n * MG_TN, (n + 1) * MG_TN)
        ya = jnp.dot(a, wa_ref[:, cols], preferred_element_type=F32)
        yb = jnp.dot(o, wb_ref[:, cols], preferred_element_type=F32)
        ga = jax.nn.sigmoid(ga_ref[:, cols].astype(F32))
        gb = jax.nn.sigmoid(gb_ref[:, cols].astype(F32))
        m_sc[:, cols] = (ga * ya + gb * yb).astype(BF16)
    x1_ref[...] = x_ref[...] + jnp.dot(m_sc[...], wo_ref[...], preferred_element_type=F32)


def _merge(a, o, z, x2, wa, wb, wo):
    m = x2.shape[0]
    once = pl.Buffered(1)
    vmem = (2 * SGU_WIDTH * D_MODEL + D_MODEL * D_MODEL) * 2 \
        + 2 * MG_TM * (2 * SGU_WIDTH * 2 + 2 * D_MODEL * 2 + 2 * D_MODEL * 4) \
        + MG_TM * D_MODEL * 2 + (8 << 20)
    return pl.pallas_call(
        _merge_kernel,
        out_shape=jax.ShapeDtypeStruct((m, D_MODEL), F32),
        grid=(m // MG_TM,),
        in_specs=[
            pl.BlockSpec((MG_TM, SGU_WIDTH), lambda i: (i, 0)),
            pl.BlockSpec((MG_TM, ATT_WIDTH), lambda i: (i, 0)),
            pl.BlockSpec((MG_TM, D_MODEL), lambda i: (i, 0)),
            pl.BlockSpec((MG_TM, D_MODEL), lambda i: (i, 1)),
            pl.BlockSpec((MG_TM, D_MODEL), lambda i: (i, 0)),
            pl.BlockSpec((SGU_WIDTH, D_MODEL), lambda i: (0, 0), pipeline_mode=once),
            pl.BlockSpec((ATT_WIDTH, D_MODEL), lambda i: (0, 0), pipeline_mode=once),
            pl.BlockSpec((D_MODEL, D_MODEL), lambda i: (0, 0), pipeline_mode=once),
        ],
        out_specs=pl.BlockSpec((MG_TM, D_MODEL), lambda i: (i, 0)),
        scratch_shapes=[pltpu.VMEM((MG_TM, D_MODEL), BF16)],
        compiler_params=pltpu.CompilerParams(
            dimension_semantics=("arbitrary",),
            vmem_limit_bytes=_vmem_limit(vmem)),
        name="merge",
    )(a, o, z, z, x2, wa, wb, wo)


FF_TM = 512
FF_TF = 512
FF_ROWS = 256


def _ffn_kernel(x_ref, gn_ref, gf_ref, wg_ref, wu_ref, wd_ref, out_ref, h_sc):
    f = pl.program_id(1)

    @pl.when(f == 0)
    def _():
        def body(r, carry):
            rows = pl.ds(pl.multiple_of(r * FF_ROWS, FF_ROWS), FF_ROWS)
            x = x_ref[rows, :]
            h_sc[rows, :] = (x * _rms_scale(x) * gn_ref[...]).astype(BF16)
            out_ref[rows, :] = x
            return carry
        lax.fori_loop(0, FF_TM // FF_ROWS, body, 0)

    h = h_sc[...]
    gate = jnp.dot(h, wg_ref[...], preferred_element_type=F32)
    up = jnp.dot(h, wu_ref[...], preferred_element_type=F32)
    act = (jax.nn.silu(gate) * up).astype(BF16)
    out_ref[...] += jnp.dot(act, wd_ref[...], preferred_element_type=F32)

    @pl.when(f == pl.num_programs(1) - 1)
    def _():
        def body(r, carry):
            rows = pl.ds(pl.multiple_of(r * FF_ROWS, FF_ROWS), FF_ROWS)
            y = out_ref[rows, :]
            out_ref[rows, :] = y * _rms_scale(y) * gf_ref[...]
            return carry
        lax.fori_loop(0, FF_TM // FF_ROWS, body, 0)


def _ffn(x1, gn, gf, wg, wu, wd):
    m = x1.shape[0]
    vmem = 2 * FF_TM * D_MODEL * 4 * 2 + FF_TM * D_MODEL * 2 \
        + 2 * 3 * D_MODEL * FF_TF * 2 + 3 * FF_TM * FF_TF * 4 + (8 << 20)
    return pl.pallas_call(
        _ffn_kernel,
        out_shape=jax.ShapeDtypeStruct((m, D_MODEL), F32),
        grid=(m // FF_TM, D_FF // FF_TF),
        in_specs=[
            pl.BlockSpec((FF_TM, D_MODEL), lambda i, f: (i, 0)),
            pl.BlockSpec((1, D_MODEL), lambda i, f: (0, 0)),
            pl.BlockSpec((1, D_MODEL), lambda i, f: (0, 0)),
            pl.BlockSpec((D_MODEL, FF_TF), lambda i, f: (0, f)),
            pl.BlockSpec((D_MODEL, FF_TF), lambda i, f: (0, f)),
            pl.BlockSpec((FF_TF, D_MODEL), lambda i, f: (f, 0)),
        ],
        out_specs=pl.BlockSpec((FF_TM, D_MODEL), lambda i, f: (i, 0)),
        scratch_shapes=[pltpu.VMEM((FF_TM, D_MODEL), BF16)],
        compiler_params=pltpu.CompilerParams(
            dimension_semantics=("arbitrary", "arbitrary"),
            vmem_limit_bytes=_vmem_limit(vmem)),
        name="ffn",
    )(x1, gn, gf, wg, wu, wd)


def kernel(x, w_in, norm_mix, sgu_v_gain, sgu_w_s, sgu_b_s, w_a_out, attn_sink, rel_bias,
           w_b_out, w_o, norm_ffn, w_gate, w_up, w_down, norm_final):
    batch, seq, d = x.shape
    assert d == D_MODEL and seq % BLOCK == 0 and w_in.shape[0] == 1
    assert (batch * seq) % IN_TM == 0 and (batch * seq) % FF_TM == 0
    x2 = x.reshape(batch * seq, D_MODEL)

    w_in_p = jnp.concatenate([w_in[0][:, MAIN_COLS:], w_in[0][:, :MAIN_COLS]], axis=1).astype(BF16)
    z = _in_proj(x2, norm_mix[0][None, :], w_in_p)

    bucket = _band_bucket_table()
    a, o = _mixers(z, bucket, rel_bias, attn_sink[0], sgu_v_gain[0][None, :],
                   sgu_w_s[0].astype(BF16), sgu_b_s[0].T, batch, seq)

    x1 = _merge(a, o, z, x2, w_a_out[0].astype(BF16), w_b_out[0].astype(BF16), w_o[0].astype(BF16))

    out = _ffn(x1, norm_ffn[0][None, :], norm_final[None, :],
               w_gate[0].astype(BF16), w_up[0].astype(BF16), w_down[0].astype(BF16))
    return out.reshape(batch, seq, D_MODEL)
```

```python
import functools
import math

import jax
import jax.numpy as jnp
import numpy as np
from jax import lax
from jax.experimental import pallas as pl
from jax.experimental.pallas import tpu as pltpu

LANES_V7X = 128
VMEM_BYTES_V7X = 64 * 1024 * 1024

D_MODEL = 2048
SGU_WIDTH = D_MODEL // 2
SGU_GROUP_DIM = 128
SGU_GROUPS = SGU_WIDTH // SGU_GROUP_DIM
CHUNK = 128
HEAD_DIM = 128
N_HEADS = (D_MODEL // 2) // HEAD_DIM
N_KV_HEADS = 2
GQA_GROUP = N_HEADS // N_KV_HEADS
ATT_WIDTH = N_HEADS * HEAD_DIM
KV_WIDTH = N_KV_HEADS * HEAD_DIM
WINDOW = 128
BLOCK = 128
BAND = 3 * BLOCK
REL_BUCKETS = 32
REL_MAX_DIST = 128
D_FF = ((8 * D_MODEL // 3 + 255) // 256) * 256
EPS = 1e-6
NEG = -1e30

OFF_U = 0
OFF_V = OFF_U + SGU_WIDTH
OFF_Q = OFF_V + SGU_WIDTH
OFF_K = OFF_Q + ATT_WIDTH
OFF_VA = OFF_K + KV_WIDTH
OFF_GA = OFF_VA + KV_WIDTH
OFF_GB = OFF_GA + D_MODEL
IN_COLS = OFF_GB + D_MODEL

BF16 = jnp.bfloat16
F32 = jnp.float32


def _vmem_limit(nbytes):
    return int(min(nbytes, VMEM_BYTES_V7X - 4 * 1024 * 1024))


def _rms_scale(x):
    return lax.rsqrt(jnp.mean(x * x, axis=-1, keepdims=True) + EPS)


IN_TM = 1024
IN_TN = 1536
IN_ROWS = 256


def _in_proj_kernel(x_ref, g_ref, w_ref, z_ref, h_sc):
    @pl.when(pl.program_id(1) == 0)
    def _():
        def body(r, carry):
            rows = pl.ds(pl.multiple_of(r * IN_ROWS, IN_ROWS), IN_ROWS)
            x = x_ref[rows, :]
            h_sc[rows, :] = (x * _rms_scale(x) * g_ref[...]).astype(BF16)
            return carry
        lax.fori_loop(0, IN_TM // IN_ROWS, body, 0)

    z_ref[...] = jnp.dot(h_sc[...], w_ref[...].astype(BF16),
                         preferred_element_type=F32).astype(BF16)


def _in_proj(x2, g, w):
    m = x2.shape[0]
    vmem = 2 * IN_TM * D_MODEL * 4 + 2 * D_MODEL * IN_TN * 4 + 2 * IN_TM * IN_TN * 2 \
        + IN_TM * D_MODEL * 2 + D_MODEL * IN_TN * 2 + (4 << 20)
    return pl.pallas_call(
        _in_proj_kernel,
        out_shape=jax.ShapeDtypeStruct((m, IN_COLS), BF16),
        grid=(m // IN_TM, IN_COLS // IN_TN),
        in_specs=[
            pl.BlockSpec((IN_TM, D_MODEL), lambda i, j: (i, 0)),
            pl.BlockSpec((1, D_MODEL), lambda i, j: (0, 0)),
            pl.BlockSpec((D_MODEL, IN_TN), lambda i, j: (0, j)),
        ],
        out_specs=pl.BlockSpec((IN_TM, IN_TN), lambda i, j: (i, j)),
        scratch_shapes=[pltpu.VMEM((IN_TM, D_MODEL), BF16)],
        compiler_params=pltpu.CompilerParams(
            dimension_semantics=("arbitrary", "arbitrary"),
            vmem_limit_bytes=_vmem_limit(vmem)),
        name="in_proj",
    )(x2, g, w)


def _band_bucket_table():
    qi = jnp.arange(BLOCK)[:, None]
    kj = jnp.arange(BAND)[None, :]
    rel = kj - BLOCK - qi
    nb = REL_BUCKETS // 2
    ret = jnp.where(rel > 0, nb, 0)
    n = jnp.abs(rel)
    max_exact = nb // 2
    nf = jnp.maximum(n, 1).astype(jnp.float32)
    large = max_exact + (jnp.log(nf / max_exact) / math.log(REL_MAX_DIST / max_exact)
                         * (nb - max_exact)).astype(jnp.int32)
    large = jnp.minimum(large, nb - 1)
    bucket = ret + jnp.where(n < max_exact, n, large)
    return jnp.where(n <= WINDOW, bucket, -1).astype(jnp.int32)


def _mixers_kernel(bucket_ref, relb_ref, sink_ref, gain_ref, ws_ref, bs_ref,
                   u_ref, v_ref, q_ref, kp_ref, kc_ref, kn_ref, vp_ref, vc_ref, vn_ref,
                   a_ref, o_ref, bias_sc, bsb_sc, kband_sc, vband_sc):
    b = pl.program_id(0)
    blk = pl.program_id(1)
    nblk = pl.num_programs(1)

    @pl.when((b == 0) & (blk == 0))
    def _():
        bucket = bucket_ref[...]
        for h in range(N_HEADS):
            acc = jnp.full((BLOCK, BAND), NEG, F32)
            for bk in range(REL_BUCKETS):
                acc = jnp.where(bucket == bk, relb_ref[bk, h], acc)
            bias_sc[h] = acc
        for g in range(SGU_GROUPS):
            bsb_sc[g] = jnp.broadcast_to(bs_ref[:, g:g + 1], (CHUNK, SGU_GROUP_DIM))

    u = jax.nn.gelu(u_ref[...].astype(F32))
    v = jax.nn.gelu(v_ref[...].astype(F32))
    vn = (v * _rms_scale(v) * gain_ref[...]).astype(BF16)
    for g in range(SGU_GROUPS):
        cols = slice(g * SGU_GROUP_DIM, (g + 1) * SGU_GROUP_DIM)
        mixed = jnp.dot(ws_ref[g], vn[:, cols], preferred_element_type=F32) + bsb_sc[g]
        a_ref[:, cols] = (u[:, cols] * mixed).astype(BF16)

    kband_sc[0:BLOCK, :] = kp_ref[...]
    kband_sc[BLOCK:2 * BLOCK, :] = kc_ref[...]
    kband_sc[2 * BLOCK:, :] = kn_ref[...]
    vband_sc[0:BLOCK, :] = vp_ref[...]
    vband_sc[BLOCK:2 * BLOCK, :] = vc_ref[...]
    vband_sc[2 * BLOCK:, :] = vn_ref[...]

    col = lax.broadcasted_iota(jnp.int32, (1, BAND), 1)
    outside = ((col < BLOCK) & (blk == 0)) | ((col >= 2 * BLOCK) & (blk == nblk - 1))
    scale = HEAD_DIM ** -0.5
    for kv in range(N_KV_HEADS):
        kcols = slice(kv * HEAD_DIM, (kv + 1) * HEAD_DIM)
        heads = range(kv * GQA_GROUP, (kv + 1) * GQA_GROUP)
        q4 = jnp.concatenate([q_ref[:, h * HEAD_DIM:(h + 1) * HEAD_DIM] for h in heads], axis=0)
        s = lax.dot_general(q4, kband_sc[:, kcols], (((1,), (1,)), ((), ())),
                            preferred_element_type=F32)
        bias4 = jnp.concatenate([bias_sc[h] for h in heads], axis=0)
        s = jnp.where(outside, NEG, s * scale + bias4)
        sink4 = jnp.concatenate([jnp.full((BLOCK, 1), sink_ref[h], F32) for h in heads], axis=0)
        mx = jnp.maximum(jnp.max(s, axis=-1, keepdims=True), sink4)
        p = jnp.exp(s - mx)
        denom = jnp.sum(p, axis=-1, keepdims=True) + jnp.exp(sink4 - mx)
        pv = jnp.dot(p.astype(BF16), vband_sc[:, kcols], preferred_element_type=F32)
        o4 = pv * (1.0 / denom)
        for n, h in enumerate(heads):
            o_ref[:, h * HEAD_DIM:(h + 1) * HEAD_DIM] = o4[n * BLOCK:(n + 1) * BLOCK].astype(BF16)


def _mixers(z, bucket, rel_bias, sink, gain, w_s, b_s_t, batch, seq):
    m = z.shape[0]
    nblk = seq // BLOCK

    def row(b, i):
        return b * nblk + i

    def zspec(width, off):
        return pl.BlockSpec((BLOCK, width), lambda b, i: (row(b, i), off // width))

    def halo(width, off, d):
        return pl.BlockSpec(
            (BLOCK, width),
            lambda b, i: (row(b, jnp.clip(i + d, 0, nblk - 1)), off // width))

    smem = pl.BlockSpec(memory_space=pltpu.SMEM)
    full = lambda shape: pl.BlockSpec(shape, lambda b, i: (0,) * len(shape))
    return pl.pallas_call(
        _mixers_kernel,
        out_shape=(jax.ShapeDtypeStruct((m, SGU_WIDTH), BF16),
                   jax.ShapeDtypeStruct((m, ATT_WIDTH), BF16)),
        grid=(batch, nblk),
        in_specs=[
            full((BLOCK, BAND)), smem, smem, full((1, SGU_WIDTH)),
            full((SGU_GROUPS, CHUNK, CHUNK)), full((CHUNK, SGU_GROUPS)),
            zspec(SGU_WIDTH, OFF_U), zspec(SGU_WIDTH, OFF_V), zspec(ATT_WIDTH, OFF_Q),
            halo(KV_WIDTH, OFF_K, -1), zspec(KV_WIDTH, OFF_K), halo(KV_WIDTH, OFF_K, 1),
            halo(KV_WIDTH, OFF_VA, -1), zspec(KV_WIDTH, OFF_VA), halo(KV_WIDTH, OFF_VA, 1),
        ],
        out_specs=(pl.BlockSpec((BLOCK, SGU_WIDTH), lambda b, i: (row(b, i), 0)),
                   pl.BlockSpec((BLOCK, ATT_WIDTH), lambda b, i: (row(b, i), 0))),
        scratch_shapes=[
            pltpu.VMEM((N_HEADS, BLOCK, BAND), F32),
            pltpu.VMEM((SGU_GROUPS, CHUNK, SGU_GROUP_DIM), F32),
            pltpu.VMEM((BAND, KV_WIDTH), BF16),
            pltpu.VMEM((BAND, KV_WIDTH), BF16),
        ],
        compiler_params=pltpu.CompilerParams(
            dimension_semantics=("arbitrary", "arbitrary"),
            vmem_limit_bytes=_vmem_limit(32 << 20)),
        name="mixers",
    )(bucket, rel_bias, sink, gain, w_s, b_s_t, z, z, z, z, z, z, z, z, z)


MG_TM = 256
MG_TN = 512


MG_NT = D_MODEL // MG_TN


def _merge_kernel(a_ref, o_ref, *refs):
    ga_refs, gb_refs = refs[:MG_NT], refs[MG_NT:2 * MG_NT]
    x_ref, wa_ref, wb_ref, wo_ref, x1_ref, m_sc = refs[2 * MG_NT:]
    a = a_ref[...]
    o = o_ref[...]
    for n in range(MG_NT):
        cols = slice(n * MG_TN, (n + 1) * MG_TN)
        ya = jnp.dot(a, wa_ref[:, cols], preferred_element_type=F32)
        yb = jnp.dot(o, wb_ref[:, cols], preferred_element_type=F32)
        ga = jax.nn.sigmoid(ga_refs[n][...].astype(F32))
        gb = jax.nn.sigmoid(gb_refs[n][...].astype(F32))
        m_sc[:, cols] = (ga * ya + gb * yb).astype(BF16)
    x1_ref[...] = x_ref[...] + jnp.dot(m_sc[...], wo_ref[...], preferred_element_type=F32)


def _merge(a, o, z, x2, wa, wb, wo):
    m = x2.shape[0]
    once = pl.Buffered(1)

    def gate_spec(off, n):
        return pl.BlockSpec((MG_TM, MG_TN), lambda i: (i, off // MG_TN + n))

    vmem = (2 * SGU_WIDTH * D_MODEL + D_MODEL * D_MODEL) * 2 \
        + 2 * MG_TM * (2 * SGU_WIDTH * 2 + 2 * D_MODEL * 2 + 2 * D_MODEL * 4) \
        + MG_TM * D_MODEL * 2 + (8 << 20)
    return pl.pallas_call(
        _merge_kernel,
        out_shape=jax.ShapeDtypeStruct((m, D_MODEL), F32),
        grid=(m // MG_TM,),
        in_specs=[
            pl.BlockSpec((MG_TM, SGU_WIDTH), lambda i: (i, 0)),
            pl.BlockSpec((MG_TM, ATT_WIDTH), lambda i: (i, 0)),
            *[gate_spec(OFF_GA, n) for n in range(MG_NT)],
            *[gate_spec(OFF_GB, n) for n in range(MG_NT)],
            pl.BlockSpec((MG_TM, D_MODEL), lambda i: (i, 0)),
            pl.BlockSpec((SGU_WIDTH, D_MODEL), lambda i: (0, 0), pipeline_mode=once),
            pl.BlockSpec((ATT_WIDTH, D_MODEL), lambda i: (0, 0), pipeline_mode=once),
            pl.BlockSpec((D_MODEL, D_MODEL), lambda i: (0, 0), pipeline_mode=once),
        ],
        out_specs=pl.BlockSpec((MG_TM, D_MODEL), lambda i: (i, 0)),
        scratch_shapes=[pltpu.VMEM((MG_TM, D_MODEL), BF16)],
        compiler_params=pltpu.CompilerParams(
            dimension_semantics=("arbitrary",),
            vmem_limit_bytes=_vmem_limit(vmem)),
        name="merge",
    )(a, o, *([z] * (2 * MG_NT)), x2, wa, wb, wo)


FF_TM = 1024
FF_TF = 256
FF_ROWS = 256


def _ffn_kernel(x_ref, gn_ref, gf_ref, wg_ref, wu_ref, wd_ref, out_ref, h_sc):
    f = pl.program_id(1)

    @pl.when(f == 0)
    def _():
        def body(r, carry):
            rows = pl.ds(pl.multiple_of(r * FF_ROWS, FF_ROWS), FF_ROWS)
            x = x_ref[rows, :]
            h_sc[rows, :] = (x * _rms_scale(x) * gn_ref[...]).astype(BF16)
            out_ref[rows, :] = x
            return carry
        lax.fori_loop(0, FF_TM // FF_ROWS, body, 0)

    h = h_sc[...]
    gate = jnp.dot(h, wg_ref[...].astype(BF16), preferred_element_type=F32)
    up = jnp.dot(h, wu_ref[...].astype(BF16), preferred_element_type=F32)
    act = (jax.nn.silu(gate) * up).astype(BF16)
    out_ref[...] += jnp.dot(act, wd_ref[...].astype(BF16), preferred_element_type=F32)

    @pl.when(f == pl.num_programs(1) - 1)
    def _():
        def body(r, carry):
            rows = pl.ds(pl.multiple_of(r * FF_ROWS, FF_ROWS), FF_ROWS)
            y = out_ref[rows, :]
            out_ref[rows, :] = y * _rms_scale(y) * gf_ref[...]
            return carry
        lax.fori_loop(0, FF_TM // FF_ROWS, body, 0)


def _ffn(x1, gn, gf, wg, wu, wd):
    m = x1.shape[0]
    vmem = 2 * FF_TM * D_MODEL * 4 * 2 + FF_TM * D_MODEL * 2 \
        + 2 * 3 * D_MODEL * FF_TF * 4 + 3 * D_MODEL * FF_TF * 2 + 3 * FF_TM * FF_TF * 4 + (4 << 20)
    return pl.pallas_call(
        _ffn_kernel,
        out_shape=jax.ShapeDtypeStruct((m, D_MODEL), F32),
        grid=(m // FF_TM, D_FF // FF_TF),
        in_specs=[
            pl.BlockSpec((FF_TM, D_MODEL), lambda i, f: (i, 0)),
            pl.BlockSpec((1, D_MODEL), lambda i, f: (0, 0)),
            pl.BlockSpec((1, D_MODEL), lambda i, f: (0, 0)),
            pl.BlockSpec((D_MODEL, FF_TF), lambda i, f: (0, f)),
            pl.BlockSpec((D_MODEL, FF_TF), lambda i, f: (0, f)),
            pl.BlockSpec((FF_TF, D_MODEL), lambda i, f: (f, 0)),
        ],
        out_specs=pl.BlockSpec((FF_TM, D_MODEL), lambda i, f: (i, 0)),
        scratch_shapes=[pltpu.VMEM((FF_TM, D_MODEL), BF16)],
        compiler_params=pltpu.CompilerParams(
            dimension_semantics=("arbitrary", "arbitrary"),
            vmem_limit_bytes=_vmem_limit(vmem)),
        name="ffn",
    )(x1, gn, gf, wg, wu, wd)


def kernel(x, w_in, norm_mix, sgu_v_gain, sgu_w_s, sgu_b_s, w_a_out, attn_sink, rel_bias,
           w_b_out, w_o, norm_ffn, w_gate, w_up, w_down, norm_final):
    batch, seq, d = x.shape
    assert d == D_MODEL and seq % BLOCK == 0 and w_in.shape[0] == 1
    assert (batch * seq) % IN_TM == 0 and (batch * seq) % FF_TM == 0
    x2 = x.reshape(batch * seq, D_MODEL)

    z = _in_proj(x2, norm_mix[0][None, :], w_in[0])

    bucket = _band_bucket_table()
    a, o = _mixers(z, bucket, rel_bias, attn_sink[0], sgu_v_gain[0][None, :],
                   sgu_w_s[0].astype(BF16), sgu_b_s[0].T, batch, seq)

    x1 = _merge(a, o, z, x2, w_a_out[0].astype(BF16), w_b_out[0].astype(BF16), w_o[0].astype(BF16))

    out = _ffn(x1, norm_ffn[0][None, :], norm_final[None, :], w_gate[0], w_up[0], w_down[0])
    return out.reshape(batch, seq, D_MODEL)
```

```python
import functools
import math

import jax
import jax.numpy as jnp
from jax import lax
from jax.experimental import pallas as pl
from jax.experimental.pallas import tpu as pltpu

VMEM_BYTES_V7X = 64 * 1024 * 1024

D_MODEL = 2048
SGU_WIDTH = D_MODEL // 2
SGU_GROUP_DIM = 128
SGU_GROUPS = SGU_WIDTH // SGU_GROUP_DIM
CHUNK = 128
HEAD_DIM = 128
N_HEADS = (D_MODEL // 2) // HEAD_DIM
N_KV_HEADS = 2
GQA_GROUP = N_HEADS // N_KV_HEADS
ATT_WIDTH = N_HEADS * HEAD_DIM
KV_WIDTH = N_KV_HEADS * HEAD_DIM
WINDOW = 128
BLOCK = 128
BAND = 3 * BLOCK
REL_BUCKETS = 32
REL_MAX_DIST = 128
D_FF = ((8 * D_MODEL // 3 + 255) // 256) * 256
EPS = 1e-6
NEG = -1e30
LOG2E = math.log2(math.e)

OFF_U = 0
OFF_V = OFF_U + SGU_WIDTH
OFF_Q = OFF_V + SGU_WIDTH
OFF_K = OFF_Q + ATT_WIDTH
OFF_VA = OFF_K + KV_WIDTH
OFF_GA = OFF_VA + KV_WIDTH
OFF_GB = OFF_GA + D_MODEL
IN_COLS = OFF_GB + D_MODEL

BF16 = jnp.bfloat16
F32 = jnp.float32


def _vmem_limit(nbytes):
    return int(min(nbytes, VMEM_BYTES_V7X - 4 * 1024 * 1024))


def _rms_scale(x):
    return lax.rsqrt(jnp.mean(x * x, axis=-1, keepdims=True) + EPS)


IN_TM = 1024
IN_TN = 1536
IN_ROWS = 256


def _in_proj_kernel(x_ref, g_ref, w_ref, z_ref, h_sc):
    @pl.when(pl.program_id(1) == 0)
    def _():
        def body(r, carry):
            rows = pl.ds(pl.multiple_of(r * IN_ROWS, IN_ROWS), IN_ROWS)
            x = x_ref[rows, :]
            h_sc[rows, :] = (x * _rms_scale(x) * g_ref[...]).astype(BF16)
            return carry
        lax.fori_loop(0, IN_TM // IN_ROWS, body, 0)

    z_ref[...] = jnp.dot(h_sc[...], w_ref[...].astype(BF16),
                         preferred_element_type=F32).astype(BF16)


def _in_proj(x2, g, w):
    m = x2.shape[0]
    vmem = 2 * IN_TM * D_MODEL * 4 + 2 * D_MODEL * IN_TN * 4 + 2 * IN_TM * IN_TN * 2 \
        + IN_TM * D_MODEL * 2 + D_MODEL * IN_TN * 2 + (4 << 20)
    return pl.pallas_call(
        _in_proj_kernel,
        out_shape=jax.ShapeDtypeStruct((m, IN_COLS), BF16),
        grid=(m // IN_TM, IN_COLS // IN_TN),
        in_specs=[
            pl.BlockSpec((IN_TM, D_MODEL), lambda i, j: (i, 0)),
            pl.BlockSpec((1, D_MODEL), lambda i, j: (0, 0)),
            pl.BlockSpec((D_MODEL, IN_TN), lambda i, j: (0, j)),
        ],
        out_specs=pl.BlockSpec((IN_TM, IN_TN), lambda i, j: (i, j)),
        scratch_shapes=[pltpu.VMEM((IN_TM, D_MODEL), BF16)],
        compiler_params=pltpu.CompilerParams(
            dimension_semantics=("arbitrary", "arbitrary"),
            vmem_limit_bytes=_vmem_limit(vmem)),
        name="in_proj",
    )(x2, g, w)


MM_TM = 256
MM_NB = MM_TM // BLOCK
MM_TN = 512
MM_NT = D_MODEL // MM_TN
BIAS_MID, BIAS_FIRST, BIAS_LAST = 0, 1, 2


def _band_bucket_table():
    qi = jnp.arange(BLOCK)[:, None]
    kj = jnp.arange(BAND)[None, :]
    rel = kj - BLOCK - qi
    nb = REL_BUCKETS // 2
    ret = jnp.where(rel > 0, nb, 0)
    n = jnp.abs(rel)
    max_exact = nb // 2
    nf = jnp.maximum(n, 1).astype(jnp.float32)
    large = max_exact + (jnp.log(nf / max_exact) / math.log(REL_MAX_DIST / max_exact)
                         * (nb - max_exact)).astype(jnp.int32)
    large = jnp.minimum(large, nb - 1)
    bucket = ret + jnp.where(n < max_exact, n, large)
    return jnp.where(n <= WINDOW, bucket, -1).astype(jnp.int32)


def _gelu_tanh(x):
    k1 = -2.0 * math.sqrt(2.0 / math.pi) * LOG2E
    k3 = k1 * 0.044715
    return x / (1.0 + jnp.exp2(x * (k1 + k3 * (x * x))))


def _init_tables(bucket_ref, relb_ref, bs_ref, bias_sc, bsb_sc):
    bucket = bucket_ref[...]
    col = lax.broadcasted_iota(jnp.int32, (BLOCK, BAND), 1)
    for h in range(N_HEADS):
        acc = jnp.full((BLOCK, BAND), NEG, F32)
        for bk in range(REL_BUCKETS):
            acc = jnp.where(bucket == bk, relb_ref[bk, h] * LOG2E, acc)
        bias_sc[BIAS_MID, h] = acc
        bias_sc[BIAS_FIRST, h] = jnp.where(col < BLOCK, NEG, acc)
        bias_sc[BIAS_LAST, h] = jnp.where(col >= 2 * BLOCK, NEG, acc)
    for g in range(SGU_GROUPS):
        bsb_sc[g] = jnp.broadcast_to(bs_ref[:, g:g + 1], (CHUNK, SGU_GROUP_DIM))


def _sgu_block(r, u_ref, v_ref, gain_ref, ws_ref, bsb_sc, a_dst):
    rows = slice(r * CHUNK, (r + 1) * CHUNK)
    u = _gelu_tanh(u_ref[rows, :].astype(F32))
    v = _gelu_tanh(v_ref[rows, :].astype(F32))
    vn = (v * _rms_scale(v) * gain_ref[...]).astype(BF16)
    for g in range(SGU_GROUPS):
        cols = slice(g * SGU_GROUP_DIM, (g + 1) * SGU_GROUP_DIM)
        mixed = jnp.dot(ws_ref[g], vn[:, cols], preferred_element_type=F32) + bsb_sc[g]
        a_dst[rows, cols] = (u[:, cols] * mixed).astype(BF16)


def _attn_block(r, variant, q_ref, kband_sc, vband_sc, bias_sc, sink_ref, o_dst):
    rows = slice(r * BLOCK, (r + 1) * BLOCK)
    band = slice(r * BLOCK, r * BLOCK + BAND)
    for kv in range(N_KV_HEADS):
        kcols = slice(kv * HEAD_DIM, (kv + 1) * HEAD_DIM)
        heads = range(kv * GQA_GROUP, (kv + 1) * GQA_GROUP)
        q4 = jnp.concatenate([q_ref[rows, h * HEAD_DIM:(h + 1) * HEAD_DIM] for h in heads], axis=0)
        s = lax.dot_general(q4, kband_sc[band, kcols], (((1,), (1,)), ((), ())),
                            preferred_element_type=F32)
        bias4 = jnp.concatenate([bias_sc[variant, h] for h in heads], axis=0)
        s = s * (HEAD_DIM ** -0.5 * LOG2E) + bias4
        sink4 = jnp.concatenate(
            [jnp.full((BLOCK, 1), sink_ref[h] * LOG2E, F32) for h in heads], axis=0)
        mx = jnp.maximum(jnp.max(s, axis=-1, keepdims=True), sink4)
        p = jnp.exp2(s - mx)
        denom = jnp.sum(p, axis=-1, keepdims=True) + jnp.exp2(sink4 - mx)
        pv = jnp.dot(p.astype(BF16), vband_sc[band, kcols], preferred_element_type=F32)
        o4 = pv * (1.0 / denom)
        for n, h in enumerate(heads):
            o_dst[rows, h * HEAD_DIM:(h + 1) * HEAD_DIM] = o4[n * BLOCK:(n + 1) * BLOCK].astype(BF16)


def _merge_tile(a_src, o_src, ga_refs, gb_refs, x_ref, wa_ref, wb_ref, wo_ref, x1_ref, m_sc):
    a = a_src[...]
    o = o_src[...]
    for n in range(MM_NT):
        cols = slice(n * MM_TN, (n + 1) * MM_TN)
        ya = jnp.dot(a, wa_ref[:, cols], preferred_element_type=F32)
        yb = jnp.dot(o, wb_ref[:, cols], preferred_element_type=F32)
        ga = jax.nn.sigmoid(ga_refs[n][...].astype(F32))
        gb = jax.nn.sigmoid(gb_refs[n][...].astype(F32))
        m_sc[:, cols] = (ga * ya + gb * yb).astype(BF16)
    x1_ref[...] = x_ref[...] + jnp.dot(m_sc[...], wo_ref[...], preferred_element_type=F32)


def _mix_merge_kernel(blocks_per_seq, bucket_ref, relb_ref, sink_ref, gain_ref, ws_ref, bs_ref,
                      u_ref, v_ref, q_ref, kp_ref, kc_ref, kn_ref, vp_ref, vc_ref, vn_ref,
                      *refs):
    ga_refs, gb_refs = refs[:MM_NT], refs[MM_NT:2 * MM_NT]
    (x_ref, wa_ref, wb_ref, wo_ref, x1_ref,
     bias_sc, bsb_sc, kband_sc, vband_sc, a_sc, o_sc, m_sc) = refs[2 * MM_NT:]
    i = pl.program_id(0)
    tile = jnp.minimum(i, pl.num_programs(0) - 2)

    @pl.when(i == 0)
    def _():
        _init_tables(bucket_ref, relb_ref, bs_ref, bias_sc, bsb_sc)
        a_sc[1] = jnp.zeros((MM_TM, SGU_WIDTH), BF16)
        o_sc[1] = jnp.zeros((MM_TM, ATT_WIDTH), BF16)

    def step(wr, rd):
        kband_sc[0:BLOCK, :] = kp_ref[...]
        kband_sc[BLOCK:BLOCK + MM_TM, :] = kc_ref[...]
        kband_sc[BLOCK + MM_TM:, :] = kn_ref[...]
        vband_sc[0:BLOCK, :] = vp_ref[...]
        vband_sc[BLOCK:BLOCK + MM_TM, :] = vc_ref[...]
        vband_sc[BLOCK + MM_TM:, :] = vn_ref[...]
        for r in range(MM_NB):
            pos = (tile * MM_NB + r) % blocks_per_seq
            variant = jnp.where(pos == 0, BIAS_FIRST,
                                jnp.where(pos == blocks_per_seq - 1, BIAS_LAST, BIAS_MID))
            _sgu_block(r, u_ref, v_ref, gain_ref, ws_ref, bsb_sc, a_sc.at[wr])
            _attn_block(r, variant, q_ref, kband_sc, vband_sc, bias_sc, sink_ref, o_sc.at[wr])
        _merge_tile(a_sc.at[rd], o_sc.at[rd], ga_refs, gb_refs, x_ref, wa_ref, wb_ref, wo_ref,
                    x1_ref, m_sc)

    parity = i % 2
    pl.when(parity == 0)(functools.partial(step, 0, 1))
    pl.when(parity == 1)(functools.partial(step, 1, 0))


def _mix_merge(z, x2, bucket, rel_bias, sink, gain, w_s, b_s_t, wa, wb, wo, seq):
    m = x2.shape[0]
    ntiles = m // MM_TM
    nblocks = m // BLOCK
    once = pl.Buffered(1)

    def mix_tile(i):
        return jnp.minimum(i, ntiles - 1)

    def merge_tile(i):
        return jnp.maximum(i - 1, 0)

    def zmix(width, off):
        return pl.BlockSpec((MM_TM, width), lambda i: (mix_tile(i), off // width))

    def halo(off, d):
        return pl.BlockSpec(
            (BLOCK, KV_WIDTH),
            lambda i: (jnp.clip(mix_tile(i) * MM_NB + d, 0, nblocks - 1), off // KV_WIDTH))

    def gate(off, n):
        return pl.BlockSpec((MM_TM, MM_TN), lambda i: (merge_tile(i), off // MM_TN + n))

    smem = pl.BlockSpec(memory_space=pltpu.SMEM)
    full = lambda shape: pl.BlockSpec(shape, lambda i: (0,) * len(shape))
    weights = (2 * SGU_WIDTH * D_MODEL + D_MODEL * D_MODEL) * 2
    tiles = 2 * MM_TM * (3 * SGU_WIDTH * 2 + 2 * KV_WIDTH * 2 + 2 * D_MODEL * 2 + 2 * D_MODEL * 4)
    scratch = 3 * N_HEADS * BLOCK * BAND * 4 + SGU_GROUPS * CHUNK * SGU_GROUP_DIM * 4 \
        + 4 * MM_TM * SGU_WIDTH * 2 + MM_TM * D_MODEL * 2
    return pl.pallas_call(
        functools.partial(_mix_merge_kernel, seq // BLOCK),
        out_shape=jax.ShapeDtypeStruct((m, D_MODEL), F32),
        grid=(ntiles + 1,),
        in_specs=[
            full((BLOCK, BAND)), smem, smem, full((1, SGU_WIDTH)),
            full((SGU_GROUPS, CHUNK, CHUNK)), full((CHUNK, SGU_GROUPS)),
            zmix(SGU_WIDTH, OFF_U), zmix(SGU_WIDTH, OFF_V), zmix(ATT_WIDTH, OFF_Q),
            halo(OFF_K, -1), zmix(KV_WIDTH, OFF_K), halo(OFF_K, MM_NB),
            halo(OFF_VA, -1), zmix(KV_WIDTH, OFF_VA), halo(OFF_VA, MM_NB),
            *[gate(OFF_GA, n) for n in range(MM_NT)],
            *[gate(OFF_GB, n) for n in range(MM_NT)],
            pl.BlockSpec((MM_TM, D_MODEL), lambda i: (merge_tile(i), 0)),
            pl.BlockSpec((SGU_WIDTH, D_MODEL), lambda i: (0, 0), pipeline_mode=once),
            pl.BlockSpec((ATT_WIDTH, D_MODEL), lambda i: (0, 0), pipeline_mode=once),
            pl.BlockSpec((D_MODEL, D_MODEL), lambda i: (0, 0), pipeline_mode=once),
        ],
        out_specs=pl.BlockSpec((MM_TM, D_MODEL), lambda i: (merge_tile(i), 0)),
        scratch_shapes=[
            pltpu.VMEM((3, N_HEADS, BLOCK, BAND), F32),
            pltpu.VMEM((SGU_GROUPS, CHUNK, SGU_GROUP_DIM), F32),
            pltpu.VMEM((BLOCK + MM_TM + BLOCK, KV_WIDTH), BF16),
            pltpu.VMEM((BLOCK + MM_TM + BLOCK, KV_WIDTH), BF16),
            pltpu.VMEM((2, MM_TM, SGU_WIDTH), BF16),
            pltpu.VMEM((2, MM_TM, ATT_WIDTH), BF16),
            pltpu.VMEM((MM_TM, D_MODEL), BF16),
        ],
        compiler_params=pltpu.CompilerParams(
            dimension_semantics=("arbitrary",),
            vmem_limit_bytes=_vmem_limit(weights + tiles + scratch + (8 << 20))),
        name="mix_merge",
    )(bucket, rel_bias, sink, gain, w_s, b_s_t, z, z, z, z, z, z, z, z, z,
      *([z] * (2 * MM_NT)), x2, wa, wb, wo)


FF_TM = 1024
FF_TF = 256
FF_ROWS = 256


def _ffn_kernel(x_ref, gn_ref, gf_ref, wg_ref, wu_ref, wd_ref, out_ref, h_sc):
    f = pl.program_id(1)

    @pl.when(f == 0)
    def _():
        def body(r, carry):
            rows = pl.ds(pl.multiple_of(r * FF_ROWS, FF_ROWS), FF_ROWS)
            x = x_ref[rows, :]
            h_sc[rows, :] = (x * _rms_scale(x) * gn_ref[...]).astype(BF16)
            out_ref[rows, :] = x
            return carry
        lax.fori_loop(0, FF_TM // FF_ROWS, body, 0)

    h = h_sc[...]
    gate = jnp.dot(h, wg_ref[...].astype(BF16), preferred_element_type=F32)
    up = jnp.dot(h, wu_ref[...].astype(BF16), preferred_element_type=F32)
    act = (jax.nn.silu(gate) * up).astype(BF16)
    out_ref[...] += jnp.dot(act, wd_ref[...].astype(BF16), preferred_element_type=F32)

    @pl.when(f == pl.num_programs(1) - 1)
    def _():
        def body(r, carry):
            rows = pl.ds(pl.multiple_of(r * FF_ROWS, FF_ROWS), FF_ROWS)
            y = out_ref[rows, :]
            out_ref[rows, :] = y * _rms_scale(y) * gf_ref[...]
            return carry
        lax.fori_loop(0, FF_TM // FF_ROWS, body, 0)


def _ffn(x1, gn, gf, wg, wu, wd):
    m = x1.shape[0]
    vmem = 2 * FF_TM * D_MODEL * 4 * 2 + FF_TM * D_MODEL * 2 \
        + 2 * 3 * D_MODEL * FF_TF * 4 + 3 * D_MODEL * FF_TF * 2 + 3 * FF_TM * FF_TF * 4 + (4 << 20)
    return pl.pallas_call(
        _ffn_kernel,
        out_shape=jax.ShapeDtypeStruct((m, D_MODEL), F32),
        grid=(m // FF_TM, D_FF // FF_TF),
        in_specs=[
            pl.BlockSpec((FF_TM, D_MODEL), lambda i, f: (i, 0)),
            pl.BlockSpec((1, D_MODEL), lambda i, f: (0, 0)),
            pl.BlockSpec((1, D_MODEL), lambda i, f: (0, 0)),
            pl.BlockSpec((D_MODEL, FF_TF), lambda i, f: (0, f)),
            pl.BlockSpec((D_MODEL, FF_TF), lambda i, f: (0, f)),
            pl.BlockSpec((FF_TF, D_MODEL), lambda i, f: (f, 0)),
        ],
        out_specs=pl.BlockSpec((FF_TM, D_MODEL), lambda i, f: (i, 0)),
        scratch_shapes=[pltpu.VMEM((FF_TM, D_MODEL), BF16)],
        compiler_params=pltpu.CompilerParams(
            dimension_semantics=("arbitrary", "arbitrary"),
            vmem_limit_bytes=_vmem_limit(vmem)),
        name="ffn",
    )(x1, gn, gf, wg, wu, wd)


def kernel(x, w_in, norm_mix, sgu_v_gain, sgu_w_s, sgu_b_s, w_a_out, attn_sink, rel_bias,
           w_b_out, w_o, norm_ffn, w_gate, w_up, w_down, norm_final):
    batch, seq, d = x.shape
    assert d == D_MODEL and seq % MM_TM == 0 and w_in.shape[0] == 1
    assert (batch * seq) % IN_TM == 0 and (batch * seq) % FF_TM == 0
    x2 = x.reshape(batch * seq, D_MODEL)

    z = _in_proj(x2, norm_mix[0][None, :], w_in[0])

    x1 = _mix_merge(z, x2, _band_bucket_table(), rel_bias, attn_sink[0], sgu_v_gain[0][None, :],
                    sgu_w_s[0].astype(BF16), sgu_b_s[0].T,
                    w_a_out[0].astype(BF16), w_b_out[0].astype(BF16), w_o[0].astype(BF16), seq)

    out = _ffn(x1, norm_ffn[0][None, :], norm_final[None, :], w_gate[0], w_up[0], w_down[0])
    return out.reshape(batch, seq, D_MODEL)
```

```python
import functools
import math

import jax
import jax.numpy as jnp
from jax import lax
from jax.experimental import pallas as pl
from jax.experimental.pallas import tpu as pltpu

VMEM_BYTES_V7X = 64 * 1024 * 1024

D_MODEL = 2048
SGU_WIDTH = D_MODEL // 2
SGU_GROUP_DIM = 128
SGU_GROUPS = SGU_WIDTH // SGU_GROUP_DIM
CHUNK = 128
HEAD_DIM = 128
N_HEADS = (D_MODEL // 2) // HEAD_DIM
N_KV_HEADS = 2
GQA_GROUP = N_HEADS // N_KV_HEADS
ATT_WIDTH = N_HEADS * HEAD_DIM
KV_WIDTH = N_KV_HEADS * HEAD_DIM
WINDOW = 128
BLOCK = 128
BAND = 3 * BLOCK
REL_BUCKETS = 32
REL_MAX_DIST = 128
D_FF = ((8 * D_MODEL // 3 + 255) // 256) * 256
EPS = 1e-6
NEG = -1e30
LOG2E = math.log2(math.e)

OFF_U = 0
OFF_V = OFF_U + SGU_WIDTH
OFF_Q = OFF_V + SGU_WIDTH
OFF_K = OFF_Q + ATT_WIDTH
OFF_VA = OFF_K + KV_WIDTH
OFF_GA = OFF_VA + KV_WIDTH
OFF_GB = OFF_GA + D_MODEL
IN_COLS = OFF_GB + D_MODEL

BF16 = jnp.bfloat16
F32 = jnp.float32


def _vmem_limit(nbytes):
    return int(min(nbytes, VMEM_BYTES_V7X - 4 * 1024 * 1024))


def _rms_scale(x):
    return lax.rsqrt(jnp.mean(x * x, axis=-1, keepdims=True) + EPS)


def _gelu_tanh(x):
    k1 = -2.0 * math.sqrt(2.0 / math.pi) * LOG2E
    k3 = k1 * 0.044715
    return x / (1.0 + jnp.exp2(x * (k1 + k3 * (x * x))))


def _sigmoid(x):
    return 1.0 / (1.0 + jnp.exp2(x * -LOG2E))


IN_TM = 1024
IN_TN = 1536
IN_ROWS = 256


def _in_proj_kernel(x_ref, g_ref, w_ref, z_ref, h_sc):
    @pl.when(pl.program_id(1) == 0)
    def _():
        def body(r, carry):
            rows = pl.ds(pl.multiple_of(r * IN_ROWS, IN_ROWS), IN_ROWS)
            x = x_ref[rows, :]
            h_sc[rows, :] = (x * _rms_scale(x) * g_ref[...]).astype(BF16)
            return carry
        lax.fori_loop(0, IN_TM // IN_ROWS, body, 0)

    z_ref[...] = jnp.dot(h_sc[...], w_ref[...].astype(BF16),
                         preferred_element_type=F32).astype(BF16)


def _in_proj(x2, g, w):
    m = x2.shape[0]
    vmem = 2 * IN_TM * D_MODEL * 4 + 2 * D_MODEL * IN_TN * 4 + 2 * IN_TM * IN_TN * 2 \
        + IN_TM * D_MODEL * 2 + D_MODEL * IN_TN * 2 + (4 << 20)
    return pl.pallas_call(
        _in_proj_kernel,
        out_shape=jax.ShapeDtypeStruct((m, IN_COLS), BF16),
        grid=(m // IN_TM, IN_COLS // IN_TN),
        in_specs=[
            pl.BlockSpec((IN_TM, D_MODEL), lambda i, j: (i, 0)),
            pl.BlockSpec((1, D_MODEL), lambda i, j: (0, 0)),
            pl.BlockSpec((D_MODEL, IN_TN), lambda i, j: (0, j)),
        ],
        out_specs=pl.BlockSpec((IN_TM, IN_TN), lambda i, j: (i, j)),
        scratch_shapes=[pltpu.VMEM((IN_TM, D_MODEL), BF16)],
        compiler_params=pltpu.CompilerParams(
            dimension_semantics=("arbitrary", "arbitrary"),
            vmem_limit_bytes=_vmem_limit(vmem)),
        name="in_proj",
    )(x2, g, w)


MM_TM = 256
MM_NB = MM_TM // BLOCK
MM_TN = 512
MM_NT = D_MODEL // MM_TN
BIAS_MID, BIAS_FIRST, BIAS_LAST = 0, 1, 2


def _band_bucket_table():
    qi = jnp.arange(BLOCK)[:, None]
    kj = jnp.arange(BAND)[None, :]
    rel = kj - BLOCK - qi
    nb = REL_BUCKETS // 2
    ret = jnp.where(rel > 0, nb, 0)
    n = jnp.abs(rel)
    max_exact = nb // 2
    nf = jnp.maximum(n, 1).astype(jnp.float32)
    large = max_exact + (jnp.log(nf / max_exact) / math.log(REL_MAX_DIST / max_exact)
                         * (nb - max_exact)).astype(jnp.int32)
    large = jnp.minimum(large, nb - 1)
    bucket = ret + jnp.where(n < max_exact, n, large)
    return jnp.where(n <= WINDOW, bucket, -1).astype(jnp.int32)


def _init_tables(bucket_ref, relb_ref, bs_ref, bias_sc, bsb_sc):
    bucket = bucket_ref[...]
    col = lax.broadcasted_iota(jnp.int32, (BLOCK, BAND), 1)
    for h in range(N_HEADS):
        acc = jnp.full((BLOCK, BAND), NEG, F32)
        for bk in range(REL_BUCKETS):
            acc = jnp.where(bucket == bk, relb_ref[bk, h] * LOG2E, acc)
        bias_sc[BIAS_MID, h] = acc
        bias_sc[BIAS_FIRST, h] = jnp.where(col < BLOCK, NEG, acc)
        bias_sc[BIAS_LAST, h] = jnp.where(col >= 2 * BLOCK, NEG, acc)
    for g in range(SGU_GROUPS):
        bsb_sc[g] = jnp.broadcast_to(bs_ref[:, g:g + 1], (CHUNK, SGU_GROUP_DIM))


class _SguBlock:
    def __init__(self, r, u_ref, v_ref, gain_ref, ws_ref, bsb_sc, a_dst):
        self.rows = slice(r * CHUNK, (r + 1) * CHUNK)
        self.u_ref, self.v_ref, self.gain_ref = u_ref, v_ref, gain_ref
        self.ws_ref, self.bsb_sc, self.a_dst = ws_ref, bsb_sc, a_dst

    def front(self):
        v = _gelu_tanh(self.v_ref[self.rows, :].astype(F32))
        self.vn = (v * _rms_scale(v) * self.gain_ref[...]).astype(BF16)

    def mix(self):
        for g in range(SGU_GROUPS):
            cols = slice(g * SGU_GROUP_DIM, (g + 1) * SGU_GROUP_DIM)
            mixed = jnp.dot(self.ws_ref[g], self.vn[:, cols], preferred_element_type=F32)
            u = _gelu_tanh(self.u_ref[self.rows, cols].astype(F32))
            self.a_dst[self.rows, cols] = (u * (mixed + self.bsb_sc[g])).astype(BF16)


class _AttnGroup:
    def __init__(self, r, kv, variant, q_ref, kband_sc, vband_sc, bias_sc, sink_ref, o_dst):
        self.rows = slice(r * BLOCK, (r + 1) * BLOCK)
        self.band = slice(r * BLOCK, r * BLOCK + BAND)
        self.kcols = slice(kv * HEAD_DIM, (kv + 1) * HEAD_DIM)
        self.heads = range(kv * GQA_GROUP, (kv + 1) * GQA_GROUP)
        self.variant, self.q_ref, self.kband_sc, self.vband_sc = variant, q_ref, kband_sc, vband_sc
        self.bias_sc, self.sink_ref, self.o_dst = bias_sc, sink_ref, o_dst

    def scores(self):
        q4 = jnp.concatenate(
            [self.q_ref[self.rows, h * HEAD_DIM:(h + 1) * HEAD_DIM] for h in self.heads], axis=0)
        self.s = lax.dot_general(q4, self.kband_sc[self.band, self.kcols],
                                 (((1,), (1,)), ((), ())), preferred_element_type=F32)

    def softmax(self):
        bias4 = jnp.concatenate([self.bias_sc[self.variant, h] for h in self.heads], axis=0)
        s = self.s * (HEAD_DIM ** -0.5 * LOG2E) + bias4
        sink4 = jnp.concatenate(
            [jnp.full((BLOCK, 1), self.sink_ref[h] * LOG2E, F32) for h in self.heads], axis=0)
        mx = jnp.maximum(jnp.max(s, axis=-1, keepdims=True), sink4)
        p = jnp.exp2(s - mx)
        self.inv = 1.0 / (jnp.sum(p, axis=-1, keepdims=True) + jnp.exp2(sink4 - mx))
        self.p = p.astype(BF16)

    def values(self):
        pv = jnp.dot(self.p, self.vband_sc[self.band, self.kcols], preferred_element_type=F32)
        o4 = pv * self.inv
        for n, h in enumerate(self.heads):
            self.o_dst[self.rows, h * HEAD_DIM:(h + 1) * HEAD_DIM] = \
                o4[n * BLOCK:(n + 1) * BLOCK].astype(BF16)


def _merge_cols(n, a_src, o_src, ga_refs, gb_refs, wa_ref, wb_ref, m_sc):
    cols = slice(n * MM_TN, (n + 1) * MM_TN)
    ya = jnp.dot(a_src[...], wa_ref[:, cols], preferred_element_type=F32)
    yb = jnp.dot(o_src[...], wb_ref[:, cols], preferred_element_type=F32)
    ga = _sigmoid(ga_refs[n][...].astype(F32))
    gb = _sigmoid(gb_refs[n][...].astype(F32))
    m_sc[:, cols] = (ga * ya + gb * yb).astype(BF16)


def _out_rows(n, x_ref, wo_ref, x1_ref, m_sc):
    cols = slice(n * MM_TN, (n + 1) * MM_TN)
    base = x_ref if n == 0 else x1_ref
    x1_ref[...] = base[...] + jnp.dot(m_sc[:, cols], wo_ref[cols, :], preferred_element_type=F32)


def _mix_merge_kernel(blocks_per_seq, bucket_ref, relb_ref, sink_ref, gain_ref, ws_ref, bs_ref,
                      u_ref, v_ref, q_ref, kp_ref, kc_ref, kn_ref, vp_ref, vc_ref, vn_ref,
                      *refs):
    ga_refs, gb_refs = refs[:MM_NT], refs[MM_NT:2 * MM_NT]
    (x_ref, wa_ref, wb_ref, wo_ref, x1_ref,
     bias_sc, bsb_sc, kband_sc, vband_sc, a_sc, o_sc, m_sc) = refs[2 * MM_NT:]
    i = pl.program_id(0)
    tile = jnp.minimum(i, pl.num_programs(0) - 2)

    @pl.when(i == 0)
    def _():
        _init_tables(bucket_ref, relb_ref, bs_ref, bias_sc, bsb_sc)
        a_sc[1] = jnp.zeros((MM_TM, SGU_WIDTH), BF16)
        o_sc[1] = jnp.zeros((MM_TM, ATT_WIDTH), BF16)

    def step(wr, rd):
        kband_sc[0:BLOCK, :] = kp_ref[...]
        kband_sc[BLOCK:BLOCK + MM_TM, :] = kc_ref[...]
        kband_sc[BLOCK + MM_TM:, :] = kn_ref[...]
        vband_sc[0:BLOCK, :] = vp_ref[...]
        vband_sc[BLOCK:BLOCK + MM_TM, :] = vc_ref[...]
        vband_sc[BLOCK + MM_TM:, :] = vn_ref[...]
        sgu, att = [], []
        for r in range(MM_NB):
            pos = (tile * MM_NB + r) % blocks_per_seq
            variant = jnp.where(pos == 0, BIAS_FIRST,
                                jnp.where(pos == blocks_per_seq - 1, BIAS_LAST, BIAS_MID))
            sgu.append(_SguBlock(r, u_ref, v_ref, gain_ref, ws_ref, bsb_sc, a_sc.at[wr]))
            att += [_AttnGroup(r, kv, variant, q_ref, kband_sc, vband_sc, bias_sc, sink_ref,
                               o_sc.at[wr]) for kv in range(N_KV_HEADS)]
        merge = [functools.partial(_merge_cols, n, a_sc.at[rd], o_sc.at[rd], ga_refs, gb_refs,
                                   wa_ref, wb_ref, m_sc) for n in range(MM_NT)]
        out = [functools.partial(_out_rows, n, x_ref, wo_ref, x1_ref, m_sc)
               for n in range(MM_NT)]
        program = [
            sgu[0].front, merge[0], sgu[0].mix, att[0].scores, att[0].softmax,
            out[0], att[0].values, att[1].scores, att[1].softmax,
            merge[1], att[1].values, sgu[1].front,
            out[1], sgu[1].mix, att[2].scores, att[2].softmax,
            merge[2], att[2].values, att[3].scores, att[3].softmax,
            out[2], att[3].values, merge[3], out[3],
        ]
        for stage in program:
            stage()

    parity = i % 2
    pl.when(parity == 0)(functools.partial(step, 0, 1))
    pl.when(parity == 1)(functools.partial(step, 1, 0))


def _mix_merge(z, x2, bucket, rel_bias, sink, gain, w_s, b_s_t, wa, wb, wo, seq):
    m = x2.shape[0]
    ntiles = m // MM_TM
    nblocks = m // BLOCK
    once = pl.Buffered(1)

    def mix_tile(i):
        return jnp.minimum(i, ntiles - 1)

    def merge_tile(i):
        return jnp.maximum(i - 1, 0)

    def zmix(width, off):
        return pl.BlockSpec((MM_TM, width), lambda i: (mix_tile(i), off // width))

    def halo(off, d):
        return pl.BlockSpec(
            (BLOCK, KV_WIDTH),
            lambda i: (jnp.clip(mix_tile(i) * MM_NB + d, 0, nblocks - 1), off // KV_WIDTH))

    def gate(off, n):
        return pl.BlockSpec((MM_TM, MM_TN), lambda i: (merge_tile(i), off // MM_TN + n))

    smem = pl.BlockSpec(memory_space=pltpu.SMEM)
    full = lambda shape: pl.BlockSpec(shape, lambda i: (0,) * len(shape))
    weights = (2 * SGU_WIDTH * D_MODEL + D_MODEL * D_MODEL) * 2
    tiles = 2 * MM_TM * (3 * SGU_WIDTH * 2 + 2 * KV_WIDTH * 2 + 2 * D_MODEL * 2 + 2 * D_MODEL * 4)
    scratch = 3 * N_HEADS * BLOCK * BAND * 4 + SGU_GROUPS * CHUNK * SGU_GROUP_DIM * 4 \
        + 4 * MM_TM * SGU_WIDTH * 2 + MM_TM * D_MODEL * 2
    return pl.pallas_call(
        functools.partial(_mix_merge_kernel, seq // BLOCK),
        out_shape=jax.ShapeDtypeStruct((m, D_MODEL), F32),
        grid=(ntiles + 1,),
        in_specs=[
            full((BLOCK, BAND)), smem, smem, full((1, SGU_WIDTH)),
            full((SGU_GROUPS, CHUNK, CHUNK)), full((CHUNK, SGU_GROUPS)),
            zmix(SGU_WIDTH, OFF_U), zmix(SGU_WIDTH, OFF_V), zmix(ATT_WIDTH, OFF_Q),
            halo(OFF_K, -1), zmix(KV_WIDTH, OFF_K), halo(OFF_K, MM_NB),
            halo(OFF_VA, -1), zmix(KV_WIDTH, OFF_VA), halo(OFF_VA, MM_NB),
            *[gate(OFF_GA, n) for n in range(MM_NT)],
            *[gate(OFF_GB, n) for n in range(MM_NT)],
            pl.BlockSpec((MM_TM, D_MODEL), lambda i: (merge_tile(i), 0)),
            pl.BlockSpec((SGU_WIDTH, D_MODEL), lambda i: (0, 0), pipeline_mode=once),
            pl.BlockSpec((ATT_WIDTH, D_MODEL), lambda i: (0, 0), pipeline_mode=once),
            pl.BlockSpec((D_MODEL, D_MODEL), lambda i: (0, 0), pipeline_mode=once),
        ],
        out_specs=pl.BlockSpec((MM_TM, D_MODEL), lambda i: (merge_tile(i), 0)),
        scratch_shapes=[
            pltpu.VMEM((3, N_HEADS, BLOCK, BAND), F32),
            pltpu.VMEM((SGU_GROUPS, CHUNK, SGU_GROUP_DIM), F32),
            pltpu.VMEM((BLOCK + MM_TM + BLOCK, KV_WIDTH), BF16),
            pltpu.VMEM((BLOCK + MM_TM + BLOCK, KV_WIDTH), BF16),
            pltpu.VMEM((2, MM_TM, SGU_WIDTH), BF16),
            pltpu.VMEM((2, MM_TM, ATT_WIDTH), BF16),
            pltpu.VMEM((MM_TM, D_MODEL), BF16),
        ],
        compiler_params=pltpu.CompilerParams(
            dimension_semantics=("arbitrary",),
            vmem_limit_bytes=_vmem_limit(weights + tiles + scratch + (8 << 20))),
        name="mix_merge",
    )(bucket, rel_bias, sink, gain, w_s, b_s_t, z, z, z, z, z, z, z, z, z,
      *([z] * (2 * MM_NT)), x2, wa, wb, wo)


FF_TM = 1024
FF_TF = 256
FF_ROWS = 256


def _ffn_kernel(x_ref, gn_ref, gf_ref, wg_ref, wu_ref, wd_ref, out_ref, h_sc):
    f = pl.program_id(1)

    @pl.when(f == 0)
    def _():
        def body(r, carry):
            rows = pl.ds(pl.multiple_of(r * FF_ROWS, FF_ROWS), FF_ROWS)
            x = x_ref[rows, :]
            h_sc[rows, :] = (x * _rms_scale(x) * gn_ref[...]).astype(BF16)
            out_ref[rows, :] = x
            return carry
        lax.fori_loop(0, FF_TM // FF_ROWS, body, 0)

    h = h_sc[...]
    gate = jnp.dot(h, wg_ref[...].astype(BF16), preferred_element_type=F32)
    up = jnp.dot(h, wu_ref[...].astype(BF16), preferred_element_type=F32)
    act = (jax.nn.silu(gate) * up).astype(BF16)
    out_ref[...] += jnp.dot(act, wd_ref[...].astype(BF16), preferred_element_type=F32)

    @pl.when(f == pl.num_programs(1) - 1)
    def _():
        def body(r, carry):
            rows = pl.ds(pl.multiple_of(r * FF_ROWS, FF_ROWS), FF_ROWS)
            y = out_ref[rows, :]
            out_ref[rows, :] = y * _rms_scale(y) * gf_ref[...]
            return carry
        lax.fori_loop(0, FF_TM // FF_ROWS, body, 0)


def _ffn(x1, gn, gf, wg, wu, wd):
    m = x1.shape[0]
    vmem = 2 * FF_TM * D_MODEL * 4 * 2 + FF_TM * D_MODEL * 2 \
        + 2 * 3 * D_MODEL * FF_TF * 4 + 3 * D_MODEL * FF_TF * 2 + 3 * FF_TM * FF_TF * 4 + (4 << 20)
    return pl.pallas_call(
        _ffn_kernel,
        out_shape=jax.ShapeDtypeStruct((m, D_MODEL), F32),
        grid=(m // FF_TM, D_FF // FF_TF),
        in_specs=[
            pl.BlockSpec((FF_TM, D_MODEL), lambda i, f: (i, 0)),
            pl.BlockSpec((1, D_MODEL), lambda i, f: (0, 0)),
            pl.BlockSpec((1, D_MODEL), lambda i, f: (0, 0)),
            pl.BlockSpec((D_MODEL, FF_TF), lambda i, f: (0, f)),
            pl.BlockSpec((D_MODEL, FF_TF), lambda i, f: (0, f)),
            pl.BlockSpec((FF_TF, D_MODEL), lambda i, f: (f, 0)),
        ],
        out_specs=pl.BlockSpec((FF_TM, D_MODEL), lambda i, f: (i, 0)),
        scratch_shapes=[pltpu.VMEM((FF_TM, D_MODEL), BF16)],
        compiler_params=pltpu.CompilerParams(
            dimension_semantics=("arbitrary", "arbitrary"),
            vmem_limit_bytes=_vmem_limit(vmem)),
        name="ffn",
    )(x1, gn, gf, wg, wu, wd)


def kernel(x, w_in, norm_mix, sgu_v_gain, sgu_w_s, sgu_b_s, w_a_out, attn_sink, rel_bias,
           w_b_out, w_o, norm_ffn, w_gate, w_up, w_down, norm_final):
    batch, seq, d = x.shape
    assert d == D_MODEL and seq % MM_TM == 0 and w_in.shape[0] == 1
    assert (batch * seq) % IN_TM == 0 and (batch * seq) % FF_TM == 0
    x2 = x.reshape(batch * seq, D_MODEL)

    z = _in_proj(x2, norm_mix[0][None, :], w_in[0])

    x1 = _mix_merge(z, x2, _band_bucket_table(), rel_bias, attn_sink[0], sgu_v_gain[0][None, :],
                    sgu_w_s[0].astype(BF16), sgu_b_s[0].T,
                    w_a_out[0].astype(BF16), w_b_out[0].astype(BF16), w_o[0].astype(BF16), seq)

    out = _ffn(x1, norm_ffn[0][None, :], norm_final[None, :], w_gate[0], w_up[0], w_down[0])
    return out.reshape(batch, seq, D_MODEL)
```

```python
import functools
import math

import jax
import jax.numpy as jnp
from jax import lax
from jax.experimental import pallas as pl
from jax.experimental.pallas import tpu as pltpu

VMEM_BYTES_V7X = 64 * 1024 * 1024

D_MODEL = 2048
SGU_WIDTH = D_MODEL // 2
SGU_GROUP_DIM = 128
SGU_GROUPS = SGU_WIDTH // SGU_GROUP_DIM
CHUNK = 128
HEAD_DIM = 128
N_HEADS = (D_MODEL // 2) // HEAD_DIM
N_KV_HEADS = 2
GQA_GROUP = N_HEADS // N_KV_HEADS
ATT_WIDTH = N_HEADS * HEAD_DIM
KV_WIDTH = N_KV_HEADS * HEAD_DIM
WINDOW = 128
BLOCK = 128
BAND = 3 * BLOCK
REL_BUCKETS = 32
REL_MAX_DIST = 128
D_FF = ((8 * D_MODEL // 3 + 255) // 256) * 256
EPS = 1e-6
NEG = -1e30
LOG2E = math.log2(math.e)

OFF_U = 0
OFF_V = OFF_U + SGU_WIDTH
OFF_Q = OFF_V + SGU_WIDTH
OFF_K = OFF_Q + ATT_WIDTH
OFF_VA = OFF_K + KV_WIDTH
OFF_GA = OFF_VA + KV_WIDTH
OFF_GB = OFF_GA + D_MODEL
IN_COLS = OFF_GB + D_MODEL

BF16 = jnp.bfloat16
F32 = jnp.float32


def _vmem_limit(nbytes):
    return int(min(nbytes, VMEM_BYTES_V7X - 4 * 1024 * 1024))


def _rms_scale(x):
    return lax.rsqrt(jnp.mean(x * x, axis=-1, keepdims=True) + EPS)


def _gelu_tanh(x):
    k1 = -2.0 * math.sqrt(2.0 / math.pi) * LOG2E
    k3 = k1 * 0.044715
    return x / (1.0 + jnp.exp2(x * (k1 + k3 * (x * x))))


def _sigmoid(x):
    return 1.0 / (1.0 + jnp.exp2(x * -LOG2E))


IN_TM = 1024
IN_TN = 1536
IN_ROWS = 256


def _in_proj_kernel(x_ref, g_ref, w_ref, z_ref, h_sc):
    @pl.when(pl.program_id(1) == 0)
    def _():
        def body(r, carry):
            rows = pl.ds(pl.multiple_of(r * IN_ROWS, IN_ROWS), IN_ROWS)
            x = x_ref[rows, :]
            h_sc[rows, :] = (x * _rms_scale(x) * g_ref[...]).astype(BF16)
            return carry
        lax.fori_loop(0, IN_TM // IN_ROWS, body, 0)

    z_ref[...] = jnp.dot(h_sc[...], w_ref[...].astype(BF16),
                         preferred_element_type=F32).astype(BF16)


def _in_proj(x2, g, w):
    m = x2.shape[0]
    vmem = 2 * IN_TM * D_MODEL * 4 + 2 * D_MODEL * IN_TN * 4 + 2 * IN_TM * IN_TN * 2 \
        + IN_TM * D_MODEL * 2 + D_MODEL * IN_TN * 2 + (4 << 20)
    return pl.pallas_call(
        _in_proj_kernel,
        out_shape=jax.ShapeDtypeStruct((m, IN_COLS), BF16),
        grid=(m // IN_TM, IN_COLS // IN_TN),
        in_specs=[
            pl.BlockSpec((IN_TM, D_MODEL), lambda i, j: (i, 0)),
            pl.BlockSpec((1, D_MODEL), lambda i, j: (0, 0)),
            pl.BlockSpec((D_MODEL, IN_TN), lambda i, j: (0, j)),
        ],
        out_specs=pl.BlockSpec((IN_TM, IN_TN), lambda i, j: (i, j)),
        scratch_shapes=[pltpu.VMEM((IN_TM, D_MODEL), BF16)],
        compiler_params=pltpu.CompilerParams(
            dimension_semantics=("arbitrary", "arbitrary"),
            vmem_limit_bytes=_vmem_limit(vmem)),
        name="in_proj",
    )(x2, g, w)


MM_TM = 256
MM_NB = MM_TM // BLOCK
MM_TN = 512
MM_NT = D_MODEL // MM_TN
BIAS_MID, BIAS_FIRST, BIAS_LAST = 0, 1, 2


def _band_bucket_table():
    qi = jnp.arange(BLOCK)[:, None]
    kj = jnp.arange(BAND)[None, :]
    rel = kj - BLOCK - qi
    nb = REL_BUCKETS // 2
    ret = jnp.where(rel > 0, nb, 0)
    n = jnp.abs(rel)
    max_exact = nb // 2
    nf = jnp.maximum(n, 1).astype(jnp.float32)
    large = max_exact + (jnp.log(nf / max_exact) / math.log(REL_MAX_DIST / max_exact)
                         * (nb - max_exact)).astype(jnp.int32)
    large = jnp.minimum(large, nb - 1)
    bucket = ret + jnp.where(n < max_exact, n, large)
    return jnp.where(n <= WINDOW, bucket, -1).astype(jnp.int32)


def _init_tables(bucket_ref, relb_ref, bs_ref, bias_sc, bsb_sc):
    bucket = bucket_ref[...]
    col = lax.broadcasted_iota(jnp.int32, (BLOCK, BAND), 1)
    for h in range(N_HEADS):
        acc = jnp.full((BLOCK, BAND), NEG, F32)
        for bk in range(REL_BUCKETS):
            acc = jnp.where(bucket == bk, relb_ref[bk, h] * LOG2E, acc)
        bias_sc[BIAS_MID, h] = acc
        bias_sc[BIAS_FIRST, h] = jnp.where(col < BLOCK, NEG, acc)
        bias_sc[BIAS_LAST, h] = jnp.where(col >= 2 * BLOCK, NEG, acc)
    for g in range(SGU_GROUPS):
        bsb_sc[g] = jnp.broadcast_to(bs_ref[:, g:g + 1], (CHUNK, SGU_GROUP_DIM))


class _SguBlock:
    def __init__(self, r, u_ref, v_ref, gain_ref, ws_ref, bsb_sc, a_dst):
        self.rows = slice(r * CHUNK, (r + 1) * CHUNK)
        self.u_ref, self.v_ref, self.gain_ref = u_ref, v_ref, gain_ref
        self.ws_ref, self.bsb_sc, self.a_dst = ws_ref, bsb_sc, a_dst

    def front(self):
        v = _gelu_tanh(self.v_ref[self.rows, :].astype(F32))
        self.vn = (v * _rms_scale(v) * self.gain_ref[...]).astype(BF16)

    def mix(self):
        for g in range(SGU_GROUPS):
            cols = slice(g * SGU_GROUP_DIM, (g + 1) * SGU_GROUP_DIM)
            mixed = jnp.dot(self.ws_ref[g], self.vn[:, cols], preferred_element_type=F32)
            u = _gelu_tanh(self.u_ref[self.rows, cols].astype(F32))
            self.a_dst[self.rows, cols] = (u * (mixed + self.bsb_sc[g])).astype(BF16)


class _AttnGroup:
    def __init__(self, r, kv, variant, q_ref, kband_sc, vband_sc, bias_sc, sink_ref, o_dst):
        self.rows = slice(r * BLOCK, (r + 1) * BLOCK)
        self.band = slice(r * BLOCK, r * BLOCK + BAND)
        self.kcols = slice(kv * HEAD_DIM, (kv + 1) * HEAD_DIM)
        self.heads = range(kv * GQA_GROUP, (kv + 1) * GQA_GROUP)
        self.variant, self.q_ref, self.kband_sc, self.vband_sc = variant, q_ref, kband_sc, vband_sc
        self.bias_sc, self.sink_ref, self.o_dst = bias_sc, sink_ref, o_dst

    def scores(self):
        q4 = jnp.concatenate(
            [self.q_ref[self.rows, h * HEAD_DIM:(h + 1) * HEAD_DIM] for h in self.heads], axis=0)
        self.s = lax.dot_general(q4, self.kband_sc[self.band, self.kcols],
                                 (((1,), (1,)), ((), ())), preferred_element_type=F32)

    def softmax(self):
        bias4 = jnp.concatenate([self.bias_sc[self.variant, h] for h in self.heads], axis=0)
        s = self.s * (HEAD_DIM ** -0.5 * LOG2E) + bias4
        sink4 = jnp.concatenate(
            [jnp.full((BLOCK, 1), self.sink_ref[h] * LOG2E, F32) for h in self.heads], axis=0)
        mx = jnp.maximum(jnp.max(s, axis=-1, keepdims=True), sink4)
        p = jnp.exp2(s - mx)
        self.inv = 1.0 / (jnp.sum(p, axis=-1, keepdims=True) + jnp.exp2(sink4 - mx))
        self.p = p.astype(BF16)

    def values(self):
        pv = jnp.dot(self.p, self.vband_sc[self.band, self.kcols], preferred_element_type=F32)
        o4 = pv * self.inv
        for n, h in enumerate(self.heads):
            self.o_dst[self.rows, h * HEAD_DIM:(h + 1) * HEAD_DIM] = \
                o4[n * BLOCK:(n + 1) * BLOCK].astype(BF16)


WS_ROWS = 256


def _stage_weights(pairs, stage, sem):
    chunks = [(src, dst, r) for src, dst in pairs for r in range(src.shape[0] // WS_ROWS)]

    def dma(k):
        src, _, r = chunks[k]
        return pltpu.make_async_copy(src.at[pl.ds(r * WS_ROWS, WS_ROWS), :],
                                     stage.at[k % 2], sem.at[k % 2])

    dma(0).start()
    for k, (_, dst, r) in enumerate(chunks):
        if k + 1 < len(chunks):
            dma(k + 1).start()
        dma(k).wait()
        dst[r * WS_ROWS // 2:(r + 1) * WS_ROWS // 2, :] = pltpu.bitcast(
            stage[k % 2].astype(BF16), jnp.uint32)


def _packed_rows(wq_ref, rows, cols):
    return pltpu.bitcast(wq_ref[rows.start // 2:rows.stop // 2, cols], BF16)


def _merge_cols(n, a_src, o_src, ga_refs, gb_refs, wa_q, wb_q, m_sc):
    cols = slice(n * MM_TN, (n + 1) * MM_TN)
    ya = jnp.dot(a_src[...], _packed_rows(wa_q, slice(0, SGU_WIDTH), cols),
                 preferred_element_type=F32)
    yb = jnp.dot(o_src[...], _packed_rows(wb_q, slice(0, ATT_WIDTH), cols),
                 preferred_element_type=F32)
    ga = _sigmoid(ga_refs[n][...].astype(F32))
    gb = _sigmoid(gb_refs[n][...].astype(F32))
    m_sc[:, cols] = (ga * ya + gb * yb).astype(BF16)


def _out_rows(n, x_ref, wo_q, x1_ref, m_sc):
    cols = slice(n * MM_TN, (n + 1) * MM_TN)
    base = x_ref if n == 0 else x1_ref
    x1_ref[...] = base[...] + jnp.dot(m_sc[:, cols], _packed_rows(wo_q, cols, slice(None)),
                                      preferred_element_type=F32)


def _mix_merge_kernel(blocks_per_seq, bucket_ref, relb_ref, sink_ref, gain_ref, ws_ref, bs_ref,
                      u_ref, v_ref, q_ref, kp_ref, kc_ref, kn_ref, vp_ref, vc_ref, vn_ref,
                      *refs):
    ga_refs, gb_refs = refs[:MM_NT], refs[MM_NT:2 * MM_NT]
    (x_ref, wa_hbm, wb_hbm, wo_hbm, x1_ref,
     bias_sc, bsb_sc, kband_sc, vband_sc, a_sc, o_sc, m_sc,
     wa_q, wb_q, wo_q, stage, sem) = refs[2 * MM_NT:]
    i = pl.program_id(0)
    tile = jnp.minimum(i, pl.num_programs(0) - 2)

    @pl.when(i == 0)
    def _():
        _stage_weights([(wa_hbm, wa_q), (wb_hbm, wb_q), (wo_hbm, wo_q)], stage, sem)
        _init_tables(bucket_ref, relb_ref, bs_ref, bias_sc, bsb_sc)
        a_sc[1] = jnp.zeros((MM_TM, SGU_WIDTH), BF16)
        o_sc[1] = jnp.zeros((MM_TM, ATT_WIDTH), BF16)

    def step(wr, rd):
        kband_sc[0:BLOCK, :] = kp_ref[...]
        kband_sc[BLOCK:BLOCK + MM_TM, :] = kc_ref[...]
        kband_sc[BLOCK + MM_TM:, :] = kn_ref[...]
        vband_sc[0:BLOCK, :] = vp_ref[...]
        vband_sc[BLOCK:BLOCK + MM_TM, :] = vc_ref[...]
        vband_sc[BLOCK + MM_TM:, :] = vn_ref[...]
        sgu, att = [], []
        for r in range(MM_NB):
            pos = (tile * MM_NB + r) % blocks_per_seq
            variant = jnp.where(pos == 0, BIAS_FIRST,
                                jnp.where(pos == blocks_per_seq - 1, BIAS_LAST, BIAS_MID))
            sgu.append(_SguBlock(r, u_ref, v_ref, gain_ref, ws_ref, bsb_sc, a_sc.at[wr]))
            att += [_AttnGroup(r, kv, variant, q_ref, kband_sc, vband_sc, bias_sc, sink_ref,
                               o_sc.at[wr]) for kv in range(N_KV_HEADS)]
        merge = [functools.partial(_merge_cols, n, a_sc.at[rd], o_sc.at[rd], ga_refs, gb_refs,
                                   wa_q, wb_q, m_sc) for n in range(MM_NT)]
        out = [functools.partial(_out_rows, n, x_ref, wo_q, x1_ref, m_sc)
               for n in range(MM_NT)]
        program = [
            sgu[0].front, merge[0], sgu[0].mix, att[0].scores, att[0].softmax,
            out[0], att[0].values, att[1].scores, att[1].softmax,
            merge[1], att[1].values, sgu[1].front,
            out[1], sgu[1].mix, att[2].scores, att[2].softmax,
            merge[2], att[2].values, att[3].scores, att[3].softmax,
            out[2], att[3].values, merge[3], out[3],
        ]
        for stage in program:
            stage()

    parity = i % 2
    pl.when(parity == 0)(functools.partial(step, 0, 1))
    pl.when(parity == 1)(functools.partial(step, 1, 0))


def _mix_merge(z, x2, bucket, rel_bias, sink, gain, w_s, b_s_t, wa, wb, wo, seq):
    m = x2.shape[0]
    ntiles = m // MM_TM
    nblocks = m // BLOCK
    assert wa.shape[0] % WS_ROWS == 0 and wb.shape[0] % WS_ROWS == 0 and wo.shape[0] % WS_ROWS == 0

    def mix_tile(i):
        return jnp.minimum(i, ntiles - 1)

    def merge_tile(i):
        return jnp.maximum(i - 1, 0)

    def zmix(width, off):
        return pl.BlockSpec((MM_TM, width), lambda i: (mix_tile(i), off // width))

    def halo(off, d):
        return pl.BlockSpec(
            (BLOCK, KV_WIDTH),
            lambda i: (jnp.clip(mix_tile(i) * MM_NB + d, 0, nblocks - 1), off // KV_WIDTH))

    def gate(off, n):
        return pl.BlockSpec((MM_TM, MM_TN), lambda i: (merge_tile(i), off // MM_TN + n))

    smem = pl.BlockSpec(memory_space=pltpu.SMEM)
    hbm = pl.BlockSpec(memory_space=pl.ANY)
    full = lambda shape: pl.BlockSpec(shape, lambda i: (0,) * len(shape))
    weights = (2 * SGU_WIDTH * D_MODEL + D_MODEL * D_MODEL) * 2 + 2 * WS_ROWS * D_MODEL * 4
    tiles = 2 * MM_TM * (3 * SGU_WIDTH * 2 + 2 * KV_WIDTH * 2 + 2 * D_MODEL * 2 + 2 * D_MODEL * 4)
    scratch = 3 * N_HEADS * BLOCK * BAND * 4 + SGU_GROUPS * CHUNK * SGU_GROUP_DIM * 4 \
        + 4 * MM_TM * SGU_WIDTH * 2 + MM_TM * D_MODEL * 2
    return pl.pallas_call(
        functools.partial(_mix_merge_kernel, seq // BLOCK),
        out_shape=jax.ShapeDtypeStruct((m, D_MODEL), F32),
        grid=(ntiles + 1,),
        in_specs=[
            full((BLOCK, BAND)), smem, smem, full((1, SGU_WIDTH)),
            full((SGU_GROUPS, CHUNK, CHUNK)), full((CHUNK, SGU_GROUPS)),
            zmix(SGU_WIDTH, OFF_U), zmix(SGU_WIDTH, OFF_V), zmix(ATT_WIDTH, OFF_Q),
            halo(OFF_K, -1), zmix(KV_WIDTH, OFF_K), halo(OFF_K, MM_NB),
            halo(OFF_VA, -1), zmix(KV_WIDTH, OFF_VA), halo(OFF_VA, MM_NB),
            *[gate(OFF_GA, n) for n in range(MM_NT)],
            *[gate(OFF_GB, n) for n in range(MM_NT)],
            pl.BlockSpec((MM_TM, D_MODEL), lambda i: (merge_tile(i), 0)),
            hbm, hbm, hbm,
        ],
        out_specs=pl.BlockSpec((MM_TM, D_MODEL), lambda i: (merge_tile(i), 0)),
        scratch_shapes=[
            pltpu.VMEM((3, N_HEADS, BLOCK, BAND), F32),
            pltpu.VMEM((SGU_GROUPS, CHUNK, SGU_GROUP_DIM), F32),
            pltpu.VMEM((BLOCK + MM_TM + BLOCK, KV_WIDTH), BF16),
            pltpu.VMEM((BLOCK + MM_TM + BLOCK, KV_WIDTH), BF16),
            pltpu.VMEM((2, MM_TM, SGU_WIDTH), BF16),
            pltpu.VMEM((2, MM_TM, ATT_WIDTH), BF16),
            pltpu.VMEM((MM_TM, D_MODEL), BF16),
            pltpu.VMEM((SGU_WIDTH // 2, D_MODEL), jnp.uint32),
            pltpu.VMEM((ATT_WIDTH // 2, D_MODEL), jnp.uint32),
            pltpu.VMEM((D_MODEL // 2, D_MODEL), jnp.uint32),
            pltpu.VMEM((2, WS_ROWS, D_MODEL), F32),
            pltpu.SemaphoreType.DMA((2,)),
        ],
        compiler_params=pltpu.CompilerParams(
            dimension_semantics=("arbitrary",),
            vmem_limit_bytes=_vmem_limit(weights + tiles + scratch + (8 << 20))),
        name="mix_merge",
    )(bucket, rel_bias, sink, gain, w_s, b_s_t, z, z, z, z, z, z, z, z, z,
      *([z] * (2 * MM_NT)), x2, wa, wb, wo)


FF_TM = 1024
FF_TF = 256
FF_ROWS = 256


def _ffn_kernel(x_ref, gn_ref, gf_ref, wg_ref, wu_ref, wd_ref, out_ref, h_sc):
    f = pl.program_id(1)

    @pl.when(f == 0)
    def _():
        def body(r, carry):
            rows = pl.ds(pl.multiple_of(r * FF_ROWS, FF_ROWS), FF_ROWS)
            x = x_ref[rows, :]
            h_sc[rows, :] = (x * _rms_scale(x) * gn_ref[...]).astype(BF16)
            out_ref[rows, :] = x
            return carry
        lax.fori_loop(0, FF_TM // FF_ROWS, body, 0)

    h = h_sc[...]
    gate = jnp.dot(h, wg_ref[...].astype(BF16), preferred_element_type=F32)
    up = jnp.dot(h, wu_ref[...].astype(BF16), preferred_element_type=F32)
    act = (jax.nn.silu(gate) * up).astype(BF16)
    out_ref[...] += jnp.dot(act, wd_ref[...].astype(BF16), preferred_element_type=F32)

    @pl.when(f == pl.num_programs(1) - 1)
    def _():
        def body(r, carry):
            rows = pl.ds(pl.multiple_of(r * FF_ROWS, FF_ROWS), FF_ROWS)
            y = out_ref[rows, :]
            out_ref[rows, :] = y * _rms_scale(y) * gf_ref[...]
            return carry
        lax.fori_loop(0, FF_TM // FF_ROWS, body, 0)


def _ffn(x1, gn, gf, wg, wu, wd):
    m = x1.shape[0]
    vmem = 2 * FF_TM * D_MODEL * 4 * 2 + FF_TM * D_MODEL * 2 \
        + 2 * 3 * D_MODEL * FF_TF * 4 + 3 * D_MODEL * FF_TF * 2 + 3 * FF_TM * FF_TF * 4 + (4 << 20)
    return pl.pallas_call(
        _ffn_kernel,
        out_shape=jax.ShapeDtypeStruct((m, D_MODEL), F32),
        grid=(m // FF_TM, D_FF // FF_TF),
        in_specs=[
            pl.BlockSpec((FF_TM, D_MODEL), lambda i, f: (i, 0)),
            pl.BlockSpec((1, D_MODEL), lambda i, f: (0, 0)),
            pl.BlockSpec((1, D_MODEL), lambda i, f: (0, 0)),
            pl.BlockSpec((D_MODEL, FF_TF), lambda i, f: (0, f)),
            pl.BlockSpec((D_MODEL, FF_TF), lambda i, f: (0, f)),
            pl.BlockSpec((FF_TF, D_MODEL), lambda i, f: (f, 0)),
        ],
        out_specs=pl.BlockSpec((FF_TM, D_MODEL), lambda i, f: (i, 0)),
        scratch_shapes=[pltpu.VMEM((FF_TM, D_MODEL), BF16)],
        compiler_params=pltpu.CompilerParams(
            dimension_semantics=("arbitrary", "arbitrary"),
            vmem_limit_bytes=_vmem_limit(vmem)),
        name="ffn",
    )(x1, gn, gf, wg, wu, wd)


def kernel(x, w_in, norm_mix, sgu_v_gain, sgu_w_s, sgu_b_s, w_a_out, attn_sink, rel_bias,
           w_b_out, w_o, norm_ffn, w_gate, w_up, w_down, norm_final):
    batch, seq, d = x.shape
    assert d == D_MODEL and seq % MM_TM == 0 and w_in.shape[0] == 1
    assert (batch * seq) % IN_TM == 0 and (batch * seq) % FF_TM == 0
    x2 = x.reshape(batch * seq, D_MODEL)

    z = _in_proj(x2, norm_mix[0][None, :], w_in[0])

    x1 = _mix_merge(z, x2, _band_bucket_table(), rel_bias, attn_sink[0], sgu_v_gain[0][None, :],
                    sgu_w_s[0].astype(BF16), sgu_b_s[0].T, w_a_out[0], w_b_out[0], w_o[0], seq)

    out = _ffn(x1, norm_ffn[0][None, :], norm_final[None, :], w_gate[0], w_up[0], w_down[0])
    return out.reshape(batch, seq, D_MODEL)
```
